```python
import jax, jax.numpy as jnp
from jax import lax
import numpy as np

D_MODEL = 1024
BATCH = 8
SEQ = 2048
DEPTH = 4
DEC_BATCH = 32
DEC_SEQ = 16
PAST_LEN = 2048

CHUNK = 64
MIX_WIDTH = D_MODEL
M_WIDTH = MIX_WIDTH // 2
S_WIDTH = MIX_WIDTH - M_WIDTH
M_HEADS = 4
M_DK = M_WIDTH // M_HEADS
M_DV = M_DK
S_GROUPS = 4
CONV_W = 3
D_FF = 4 * D_MODEL
N_MOD = 6
EPS = 1e-6

OFF_Q = 0
OFF_K = OFF_Q + M_WIDTH
OFF_V = OFF_K + M_WIDTH
OFF_O = OFF_V + M_WIDTH
OFF_I = OFF_O + M_WIDTH
OFF_F = OFF_I + M_HEADS
OFF_B = OFF_F + M_HEADS
OFF_C = OFF_B + S_WIDTH
OFF_X = OFF_C + S_WIDTH
N_IN = OFF_X + S_WIDTH

kernel_name = "hybrid_mlstm_shortconv_streaming_step"


def _rms(x):
    x32 = x.astype(jnp.float32)
    return (x32 * lax.rsqrt(jnp.mean(jnp.square(x32), axis=-1, keepdims=True) + EPS)).astype(x.dtype)


def _group_rms(x, groups):
    shp = x.shape
    xg = x.reshape(shp[:-1] + (groups, shp[-1] // groups))
    return _rms(xg).reshape(shp)


def _mlstm_chunk(carry, inp):
    C0, n0, m0 = carry
    q, k, v, logi, logf = inp
    L = q.shape[2]
    b = jnp.cumsum(logf, axis=-1)
    a = logi - b
    m = b + jnp.maximum(m0[..., None], lax.cummax(a, axis=2))
    causal = jnp.tril(jnp.ones((L, L), dtype=bool))
    logw = (b - m)[..., :, None] + a[..., None, :]
    w = jnp.exp(jnp.where(causal, logw, -jnp.inf))
    g = jnp.exp(b + m0[..., None] - m)
    s = jnp.einsum('bhtd,bhsd->bhts', q, k) * w
    num = g[..., None] * jnp.einsum('bhtd,bhde->bhte', q, C0) + jnp.einsum('bhts,bhse->bhte', s, v)
    den = g * jnp.einsum('bhtd,bhd->bht', q, n0) + jnp.sum(s, axis=-1)
    h = num / jnp.maximum(jnp.abs(den), jnp.exp(-m))[..., None]
    mL = m[..., -1]
    wend = jnp.exp(b[..., -1:] - mL[..., None] + a)
    decay = jnp.exp(b[..., -1] + m0 - mL)
    C1 = decay[..., None, None] * C0 + jnp.einsum('bhs,bhsd,bhse->bhde', wend, k, v)
    n1 = decay[..., None] * n0 + jnp.einsum('bhs,bhsd->bhd', wend, k)
    return (C1, n1, mL), h


def _mlstm(q, k, v, logi, logf, C0, n0, m0):
    Bsz, T, H = q.shape[0], q.shape[1], q.shape[2]
    L = CHUNK if T % CHUNK == 0 else T
    NC = T // L
    f32 = jnp.float32

    def blocks(t):
        t = t.astype(f32).reshape((Bsz, NC, L, H) + t.shape[3:])
        return jnp.moveaxis(t, (1, 3), (0, 2))

    carry0 = (C0.astype(f32), n0.astype(f32), m0.astype(f32))
    carry, h = lax.scan(_mlstm_chunk, carry0, (blocks(q), blocks(k), blocks(v), blocks(logi), blocks(logf)))
    h = jnp.moveaxis(h, (0, 2), (1, 3)).reshape(Bsz, T, H, M_DV)
    return h, carry


def _layer(x, c, C0, n0, m0, conv_prev, w_ada, b_ada, g1, w_in, b_in, conv_w, g_mix, w_out, g2, w_up, w_down):
    Bsz, T, _ = x.shape
    mod = jax.nn.silu(c) @ w_ada + b_ada
    sh1, sc1, gt1, sh2, sc2, gt2 = jnp.split(mod[:, None, :], N_MOD, axis=-1)
    h = _rms(x) * g1 * (1 + sc1) + sh1
    z = h @ w_in + b_in
    q = z[..., OFF_Q:OFF_K].reshape(Bsz, T, M_HEADS, M_DK)
    k = z[..., OFF_K:OFF_V].reshape(Bsz, T, M_HEADS, M_DK) * (M_DK ** -0.5)
    v = z[..., OFF_V:OFF_O].reshape(Bsz, T, M_HEADS, M_DV)
    o_pre = z[..., OFF_O:OFF_I]
    logi = z[..., OFF_I:OFF_F].astype(jnp.float32)
    logf = jax.nn.log_sigmoid(z[..., OFF_F:OFF_B].astype(jnp.float32))
    hm, (C1, n1, m1) = _mlstm(q, k, v, logi, logf, C0, n0, m0)
    hm = _group_rms(hm.astype(x.dtype).reshape(Bsz, T, M_WIDTH), M_HEADS) * jax.nn.sigmoid(o_pre)
    bg = z[..., OFF_B:OFF_C]
    u = z[..., OFF_C:OFF_X] * z[..., OFF_X:N_IN]
    ext = jnp.concatenate([conv_prev.astype(u.dtype), u], axis=1)
    yc = conv_w[0] * ext[:, 0:T]
    for j in range(1, CONV_W):
        yc = yc + conv_w[j] * ext[:, j:j + T]
    ys = _group_rms(bg * yc, S_GROUPS)
    mix = jnp.concatenate([hm, ys], axis=-1) * g_mix
    x = x + gt1 * (mix @ w_out)
    h2 = _rms(x) * g2 * (1 + sc2) + sh2
    x = x + gt2 * (jnp.square(jax.nn.relu(h2 @ w_up)) @ w_down)
    return x, (C1, n1, m1, ext[:, -(CONV_W - 1):])


def setup_inputs(seed: int = 0) -> dict:
    key = jax.random.key(seed)
    ks = jax.random.split(key, 20)
    d = D_MODEL

    def nrm(k, shape, s):
        return s * jax.random.normal(k, shape, jnp.float32)

    b_in = nrm(ks[12], (DEPTH, N_IN), 0.02).at[:, OFF_F:OFF_B].add(jnp.linspace(3.0, 6.0, M_HEADS))
    return {
        "x_prompt": nrm(ks[0], (BATCH, SEQ, d), 1.0),
        "x_sample": nrm(ks[1], (DEC_BATCH, DEC_SEQ, d), 1.0),
        "c_prompt": nrm(ks[2], (BATCH, d), 1.0),
        "c_sample": nrm(ks[3], (DEC_BATCH, d), 1.0),
        "state_C": nrm(ks[4], (DEPTH, DEC_BATCH, M_HEADS, M_DK, M_DV), 0.1),
        "state_n": nrm(ks[5], (DEPTH, DEC_BATCH, M_HEADS, M_DK), 0.1),
        "state_m": nrm(ks[6], (DEPTH, DEC_BATCH, M_HEADS), 0.5),
        "state_conv": nrm(ks[7], (DEPTH, DEC_BATCH, CONV_W - 1, S_WIDTH), 1.0),
        "w_ada": nrm(ks[8], (DEPTH, d, N_MOD * d), 0.5 * d ** -0.5),
        "b_ada": nrm(ks[9], (DEPTH, N_MOD * d), 0.02),
        "g_norm1": 1.0 + nrm(ks[10], (DEPTH, d), 0.02),
        "w_in": nrm(ks[11], (DEPTH, d, N_IN), d ** -0.5),
        "b_in": b_in,
        "conv_w": nrm(ks[13], (DEPTH, CONV_W, S_WIDTH), CONV_W ** -0.5),
        "g_mix_out": 1.0 + nrm(ks[14], (DEPTH, MIX_WIDTH), 0.02),
        "w_out": nrm(ks[15], (DEPTH, MIX_WIDTH, d), MIX_WIDTH ** -0.5),
        "g_norm2": 1.0 + nrm(ks[16], (DEPTH, d), 0.02),
        "w_up": nrm(ks[17], (DEPTH, d, D_FF), d ** -0.5),
        "w_down": nrm(ks[18], (DEPTH, D_FF, d), D_FF ** -0.5),
        "g_final": 1.0 + nrm(ks[19], (d,), 0.02),
    }


def reference(x_prompt, x_sample, c_prompt, c_sample, state_C, state_n, state_m, state_conv,
              w_ada, b_ada, g_norm1, w_in, b_in, conv_w, g_mix_out, w_out, g_norm2, w_up, w_down, g_final):
    xp, xs = x_prompt, x_sample
    bp = xp.shape[0]
    C0p = jnp.zeros((bp, M_HEADS, M_DK, M_DV), jnp.float32)
    n0p = jnp.zeros((bp, M_HEADS, M_DK), jnp.float32)
    m0p = jnp.zeros((bp, M_HEADS), jnp.float32)
    cv0p = jnp.zeros((bp, CONV_W - 1, S_WIDTH), xp.dtype)
    pC, pn, pm, pcv, sC, sn, sm, scv = [], [], [], [], [], [], [], []
    for l in range(DEPTH):
        wl = (w_ada[l], b_ada[l], g_norm1[l], w_in[l], b_in[l], conv_w[l], g_mix_out[l],
              w_out[l], g_norm2[l], w_up[l], w_down[l])
        xp, (C1, n1, m1, cb) = _layer(xp, c_prompt, C0p, n0p, m0p, cv0p, *wl)
        pC.append(C1); pn.append(n1); pm.append(m1); pcv.append(cb)
        xs, (C2, n2, m2, cb2) = _layer(xs, c_sample, state_C[l], state_n[l], state_m[l], state_conv[l], *wl)
        sC.append(C2); sn.append(n2); sm.append(m2); scv.append(cb2)
    y_prompt = _rms(xp) * g_final
    y_sample = _rms(xs) * g_final
    return (y_prompt, y_sample,
            jnp.stack(pC), jnp.stack(pn), jnp.stack(pm), jnp.stack(pcv),
            jnp.stack(sC), jnp.stack(sn), jnp.stack(sm), jnp.stack(scv))
```

```python
import functools

import jax
import jax.numpy as jnp
from jax import lax
from jax.experimental import pallas as pl
from jax.experimental.pallas import tpu as pltpu

F32 = jnp.float32
BF16 = jnp.bfloat16

D_MODEL = 1024
DEPTH = 4
M_HEADS = 4
M_DK = 128
M_WIDTH = M_HEADS * M_DK
S_WIDTH = 512
S_GROUPS = 4
CONV_W = 3
D_FF = 4 * D_MODEL
N_MOD = 6
EPS = 1e-6

REF_OFF_I = 4 * M_WIDTH
REF_OFF_F = REF_OFF_I + M_HEADS
REF_OFF_B = REF_OFF_F + M_HEADS
REF_N_IN = REF_OFF_B + 3 * S_WIDTH

LANES = 128
BLK = 128
OFF_Q, OFF_K, OFF_V, OFF_O = 0, 512, 1024, 1536
OFF_B, OFF_C, OFF_X = 2048, 2560, 3072
OFF_GI, OFF_GF = 3584, 3712
N_Z = 3840
FF_CHUNK = 1024
VMEM_LIMIT = 56 * 1024 * 1024


def _rms_rows(x):
    return x * lax.rsqrt(jnp.mean(x * x, axis=-1, keepdims=True) + EPS)


def _log_sigmoid(x):
    return jnp.minimum(x, 0.0) - jnp.log1p(jnp.exp(-jnp.abs(x)))


def _mod_kernel(c_ref, w_ref, b_ref, o_ref):
    c = c_ref[...]
    a = (c * jax.nn.sigmoid(c)).astype(BF16)
    o_ref[...] = jnp.dot(a, w_ref[...].astype(BF16), preferred_element_type=F32) + b_ref[...]


def _modulation(c_all, w_ada, b_ada):
    nrow = c_all.shape[0]
    b4 = b_ada.reshape(DEPTH, N_MOD, 1, D_MODEL)
    out = pl.pallas_call(
        _mod_kernel,
        out_shape=jax.ShapeDtypeStruct((DEPTH, N_MOD, nrow, D_MODEL), F32),
        grid=(DEPTH, N_MOD),
        in_specs=[
            pl.BlockSpec((nrow, D_MODEL), lambda l, j: (0, 0)),
            pl.BlockSpec((None, D_MODEL, D_MODEL), lambda l, j: (l, 0, j)),
            pl.BlockSpec((None, None, 1, D_MODEL), lambda l, j: (l, j, 0, 0)),
        ],
        out_specs=pl.BlockSpec((None, None, nrow, D_MODEL), lambda l, j: (l, j, 0, 0)),
        compiler_params=pltpu.CompilerParams(dimension_semantics=("arbitrary", "arbitrary")),
        name="adaln_modulation",
    )(c_all, w_ada, b4)
    return out.reshape(DEPTH, N_MOD, nrow, 1, D_MODEL)


def _mixer_kernel(x_ref, sh_ref, sc_ref, gt_ref, g1_ref, win_ref, bin_ref, cw_ref, gmix_ref, wout_ref,
                  c0_ref, n0_ref, m0_ref, cv0_ref,
                  xo_ref, c_ref, n_ref, m_ref, cv_ref,
                  q_s, k_s, v_s, og_s, mix_s, ubuf_s, p_s, g_s, we_s, em_s, rt_s,
                  *, nb, tm, seg, loop_blocks):
    rows = nb * tm
    nblk = rows // BLK
    nseg = BLK // seg

    @pl.when(pl.program_id(1) == 0)
    def _():
        c_ref[...] = c0_ref[...]
        n_ref[...] = n0_ref[...]
        m_ref[...] = m0_ref[...]
        cv_ref[...] = cv0_ref[...]

    x3 = x_ref[...]
    h3 = _rms_rows(x3) * (g1_ref[...] * (1.0 + sc_ref[...])) + sh_ref[...]
    hb = h3.reshape(rows, D_MODEL).astype(BF16)

    def proj(off, width):
        return (jnp.dot(hb, win_ref[:, off:off + width], preferred_element_type=F32)
                + bin_ref[:, off:off + width])

    q_s[...] = proj(OFF_Q, M_WIDTH).astype(BF16)
    k_s[...] = (proj(OFF_K, M_WIDTH) * (M_DK ** -0.5)).astype(BF16)
    v_s[...] = proj(OFF_V, M_WIDTH).astype(BF16)
    og_s[...] = jax.nn.sigmoid(proj(OFF_O, M_WIDTH))

    bg = proj(OFF_B, S_WIDTH)
    u = proj(OFF_C, S_WIDTH) * proj(OFF_X, S_WIDTH)
    ubuf_s[:, 6:8, :] = cv_ref[...]
    ubuf_s[:, 8:8 + tm, :] = u.reshape(nb, tm, S_WIDTH)
    cw = cw_ref[...]
    yc = (cw[0:1] * ubuf_s[:, 6:6 + tm, :] + cw[1:2] * ubuf_s[:, 7:7 + tm, :]
          + cw[2:3] * ubuf_s[:, 8:8 + tm, :])
    cv_ref[...] = ubuf_s[:, tm + 6:tm + 8, :]
    ysv = (bg.reshape(nb, tm, S_WIDTH) * yc).reshape(rows, S_WIDTH)
    for grp in range(S_GROUPS):
        lo = grp * LANES
        blk = _rms_rows(ysv[:, lo:lo + LANES])
        mix_s[:, M_WIDTH + lo:M_WIDTH + lo + LANES] = (
            blk * gmix_ref[:, M_WIDTH + lo:M_WIDTH + lo + LANES]).astype(BF16)

    gates = proj(OFF_GI, 2 * LANES)
    logi = gates[:, :LANES]
    logf = _log_sigmoid(gates[:, LANES:])
    rin = lax.broadcasted_iota(jnp.int32, (rows, LANES), 0) & (tm - 1)
    bt = logf
    shift = 1
    while shift < tm:
        bt = bt + jnp.where(rin >= shift, pltpu.roll(bt, shift, axis=0), 0.0)
        shift *= 2
    r = logi - bt
    cm = r
    shift = 1
    while shift < tm:
        cm = jnp.maximum(cm, jnp.where(rin >= shift, pltpu.roll(cm, shift, axis=0), -jnp.inf))
        shift *= 2
    m0b = jnp.broadcast_to(m_ref[...], (nb, tm, LANES)).reshape(rows, LANES)
    cmx = jnp.maximum(m0b, cm)
    p = -cmx
    m = bt + cmx
    m_ref[...] = m.reshape(nb, tm, LANES)[:, tm - 1:tm, :]
    nchunk = rows // seg
    p3 = p.reshape(nchunk, seg, LANES)
    p_before = jnp.where(rin == 0, -m0b, pltpu.roll(p, 1, axis=0)).reshape(nchunk, seg, LANES)[:, 0:1, :]
    p_s[...] = p
    g_s[...] = jnp.exp(p3 - p_before).reshape(rows, LANES)
    we_s[...] = jnp.exp(p3[:, seg - 1:seg, :] + r.reshape(nchunk, seg, LANES)).reshape(rows, LANES)
    em_s[...] = jnp.exp(-m)
    for bi in range(nblk):
        rt_s[:, bi * BLK:(bi + 1) * BLK] = r[bi * BLK:(bi + 1) * BLK, :].T

    ri = lax.broadcasted_iota(jnp.int32, (BLK, BLK), 0)
    ci = lax.broadcasted_iota(jnp.int32, (BLK, BLK), 1)
    mask = ci <= ri
    if nseg > 1:
        sh = seg.bit_length() - 1
        mask = mask & ((ri >> sh) == (ci >> sh))

    def block_body(bi):
        r0 = bi * BLK
        if not isinstance(bi, int):
            r0 = pl.multiple_of(r0, BLK)
        rsl = pl.ds(r0, BLK)
        for h in range(M_HEADS):
            hs = slice(h * M_DK, (h + 1) * M_DK)
            qh = q_s[rsl, hs]
            kh = k_s[rsl, hs]
            vh = v_s[rsl, hs]
            s = lax.dot_general(qh, kh, (((1,), (1,)), ((), ())), preferred_element_type=F32)
            logw = p_s[rsl, h:h + 1] + rt_s[h:h + 1, pl.ds(r0, BLK)]
            sw = s * jnp.exp(jnp.where(mask, logw, -jnp.inf))
            rowsum = jnp.sum(sw, axis=-1, keepdims=True)
            intra = jnp.dot(sw.astype(BF16), vh, preferred_element_type=F32)
            gc = g_s[rsl, h:h + 1]
            ec = em_s[rsl, h:h + 1]
            kw = kh.astype(F32) * we_s[rsl, h:h + 1]
            qf = qh.astype(F32)
            for j in range(nseg):
                if tm >= BLK:
                    seq = r0 // tm
                else:
                    seq = bi * (BLK // tm) + j
                sl = slice(j * seg, (j + 1) * seg)
                rj = r0 + j * seg
                if not isinstance(bi, int):
                    rj = pl.multiple_of(rj, seg)
                c0 = c_ref[seq, h]
                n0 = n_ref[seq, h:h + 1, :]
                inter = jnp.dot(qh[sl], c0.astype(BF16), preferred_element_type=F32)
                qn = jnp.sum(qf[sl] * n0, axis=-1, keepdims=True)
                num = gc[sl] * inter + intra[sl]
                den = gc[sl] * qn + rowsum[sl]
                hh = num * (1.0 / jnp.maximum(jnp.abs(den), ec[sl]))
                out = _rms_rows(hh) * og_s[pl.ds(rj, seg), hs] * gmix_ref[:, hs]
                mix_s[pl.ds(rj, seg), hs] = out.astype(BF16)
                decay = gc[sl][seg - 1:seg, :]
                c_ref[seq, h] = decay * c0 + lax.dot_general(
                    kw[sl].astype(BF16), vh[sl], (((0,), (0,)), ((), ())), preferred_element_type=F32)
                n_ref[seq, h:h + 1, :] = decay * n0 + jnp.sum(kw[sl], axis=0, keepdims=True)

    if loop_blocks:
        def fbody(bi, carry):
            block_body(bi)
            return carry
        lax.fori_loop(0, nblk, fbody, 0)
    else:
        for bi in range(nblk):
            block_body(bi)

    y = jnp.dot(mix_s[...], wout_ref[...], preferred_element_type=F32)
    xo_ref[...] = x3 + gt_ref[...] * y.reshape(nb, tm, D_MODEL)


def _mixer(x, mod, mod_row0, layer, w, state, *, nb, tm, name):
    bsz, t_len, _ = x.shape
    rows = nb * tm
    seg = min(BLK, tm)
    assert rows % BLK == 0 and bsz % nb == 0 and t_len % tm == 0 and mod_row0 % nb == 0
    assert tm & (tm - 1) == 0 and (tm >= BLK or nb * tm >= BLK)
    c0, n0, m0, cv0, st_layer = state
    grid = (bsz // nb, t_len // tm)
    mrow = mod_row0 // nb

    def mod_spec(j):
        return pl.BlockSpec((None, None, nb, 1, D_MODEL), lambda b, t: (layer, j, mrow + b, 0, 0))

    def wspec(shape):
        nd = len(shape)
        return pl.BlockSpec((None,) + shape, lambda b, t: (layer,) + (0,) * nd)

    in_specs = [
        pl.BlockSpec((nb, tm, D_MODEL), lambda b, t: (b, t, 0)),
        mod_spec(0), mod_spec(1), mod_spec(2),
        wspec((1, D_MODEL)),
        wspec((D_MODEL, N_Z)),
        wspec((1, N_Z)),
        wspec((CONV_W, S_WIDTH)),
        wspec((1, D_MODEL)),
        wspec((D_MODEL, D_MODEL)),
        pl.BlockSpec((None, nb, M_HEADS, M_DK, M_DK), lambda b, t: (st_layer, b, 0, 0, 0)),
        pl.BlockSpec((None, nb, M_HEADS, M_DK), lambda b, t: (st_layer, b, 0, 0)),
        pl.BlockSpec((None, nb, 1, LANES), lambda b, t: (st_layer, b, 0, 0)),
        pl.BlockSpec((None, nb, CONV_W - 1, S_WIDTH), lambda b, t: (st_layer, b, 0, 0)),
    ]
    out_shape = (
        jax.ShapeDtypeStruct(x.shape, F32),
        jax.ShapeDtypeStruct((bsz, M_HEADS, M_DK, M_DK), F32),
        jax.ShapeDtypeStruct((bsz, M_HEADS, M_DK), F32),
        jax.ShapeDtypeStruct((bsz, 1, LANES), F32),
        jax.ShapeDtypeStruct((bsz, CONV_W - 1, S_WIDTH), F32),
    )
    out_specs = (
        pl.BlockSpec((nb, tm, D_MODEL), lambda b, t: (b, t, 0)),
        pl.BlockSpec((nb, M_HEADS, M_DK, M_DK), lambda b, t: (b, 0, 0, 0)),
        pl.BlockSpec((nb, M_HEADS, M_DK), lambda b, t: (b, 0, 0)),
        pl.BlockSpec((nb, 1, LANES), lambda b, t: (b, 0, 0)),
        pl.BlockSpec((nb, CONV_W - 1, S_WIDTH), lambda b, t: (b, 0, 0)),
    )
    scratch = [
        pltpu.VMEM((rows, M_WIDTH), BF16),
        pltpu.VMEM((rows, M_WIDTH), BF16),
        pltpu.VMEM((rows, M_WIDTH), BF16),
        pltpu.VMEM((rows, M_WIDTH), F32),
        pltpu.VMEM((rows, D_MODEL), BF16),
        pltpu.VMEM((nb, tm + 8, S_WIDTH), F32),
        pltpu.VMEM((rows, LANES), F32),
        pltpu.VMEM((rows, LANES), F32),
        pltpu.VMEM((rows, LANES), F32),
        pltpu.VMEM((rows, LANES), F32),
        pltpu.VMEM((LANES, rows), F32),
    ]
    kern = functools.partial(_mixer_kernel, nb=nb, tm=tm, seg=seg, loop_blocks=(tm < BLK))
    return pl.pallas_call(
        kern, out_shape=out_shape, grid=grid, in_specs=in_specs, out_specs=out_specs,
        scratch_shapes=scratch,
        compiler_params=pltpu.CompilerParams(dimension_semantics=("arbitrary", "arbitrary"),
                                             vmem_limit_bytes=VMEM_LIMIT),
        name=name,
    )(x, mod, mod, mod, w["g1"], w["w_in"], w["b_in"], w["conv_w"], w["g_mix"], w["w_out"],
      c0, n0, m0, cv0)


def _mlp_kernel(x_ref, sh_ref, sc_ref, gt_ref, g2_ref, wup_ref, wdn_ref, gf_ref, xo_ref, *, nb, tm, final):
    rows = nb * tm
    x3 = x_ref[...]
    h3 = _rms_rows(x3) * (g2_ref[...] * (1.0 + sc_ref[...])) + sh_ref[...]
    hb = h3.reshape(rows, D_MODEL).astype(BF16)
    acc = jnp.zeros((rows, D_MODEL), F32)
    for c in range(D_FF // FF_CHUNK):
        lo = c * FF_CHUNK
        up = jnp.dot(hb, wup_ref[:, lo:lo + FF_CHUNK], preferred_element_type=F32)
        act = jnp.square(jnp.maximum(up, 0.0)).astype(BF16)
        acc = acc + jnp.dot(act, wdn_ref[lo:lo + FF_CHUNK, :], preferred_element_type=F32)
    xn = x3 + gt_ref[...] * acc.reshape(nb, tm, D_MODEL)
    if final:
        xn = _rms_rows(xn) * gf_ref[...]
    xo_ref[...] = xn


def _mlp(x, mod, mod_row0, layer, w, g_final, *, nb, tm, final, name):
    bsz, t_len, _ = x.shape
    assert bsz % nb == 0 and t_len % tm == 0 and mod_row0 % nb == 0
    grid = (bsz // nb, t_len // tm)
    mrow = mod_row0 // nb

    def mod_spec(j):
        return pl.BlockSpec((None, None, nb, 1, D_MODEL), lambda b, t: (layer, j, mrow + b, 0, 0))

    in_specs = [
        pl.BlockSpec((nb, tm, D_MODEL), lambda b, t: (b, t, 0)),
        mod_spec(3), mod_spec(4), mod_spec(5),
        pl.BlockSpec((None, 1, D_MODEL), lambda b, t: (layer, 0, 0)),
        pl.BlockSpec((None, D_MODEL, D_FF), lambda b, t: (layer, 0, 0)),
        pl.BlockSpec((None, D_FF, D_MODEL), lambda b, t: (layer, 0, 0)),
        pl.BlockSpec((1, D_MODEL), lambda b, t: (0, 0)),
    ]
    kern = functools.partial(_mlp_kernel, nb=nb, tm=tm, final=final)
    return pl.pallas_call(
        kern, out_shape=jax.ShapeDtypeStruct(x.shape, F32), grid=grid, in_specs=in_specs,
        out_specs=pl.BlockSpec((nb, tm, D_MODEL), lambda b, t: (b, t, 0)),
        compiler_params=pltpu.CompilerParams(dimension_semantics=("arbitrary", "arbitrary"),
                                             vmem_limit_bytes=VMEM_LIMIT),
        name=name,
    )(x, mod, mod, mod, w["g2"], w["w_up"], w["w_down"], g_final)


def kernel(x_prompt, x_sample, c_prompt, c_sample, state_C, state_n, state_m, state_conv, w_ada, b_ada,
           g_norm1, w_in, b_in, conv_w, g_mix_out, w_out, g_norm2, w_up, w_down, g_final):
    bp, sp, _ = x_prompt.shape
    bs, ss, _ = x_sample.shape

    def relayout(a):
        gi = a[..., REF_OFF_I:REF_OFF_F]
        gf = a[..., REF_OFF_F:REF_OFF_B]
        pad = [(0, 0)] * (a.ndim - 1) + [(0, LANES - M_HEADS)]
        return jnp.concatenate([a[..., :REF_OFF_I], a[..., REF_OFF_B:], jnp.pad(gi, pad), jnp.pad(gf, pad)], axis=-1)

    w = {
        "g1": g_norm1.reshape(DEPTH, 1, D_MODEL),
        "w_in": relayout(w_in).astype(BF16),
        "b_in": relayout(b_in).reshape(DEPTH, 1, N_Z),
        "conv_w": conv_w,
        "g_mix": g_mix_out.reshape(DEPTH, 1, D_MODEL),
        "w_out": w_out.astype(BF16),
        "g2": g_norm2.reshape(DEPTH, 1, D_MODEL),
        "w_up": w_up.astype(BF16),
        "w_down": w_down.astype(BF16),
    }
    gfin = g_final.reshape(1, D_MODEL)

    mod = _modulation(jnp.concatenate([c_sample, c_prompt], axis=0), w_ada, b_ada)

    zero_state = (jnp.zeros((1, bp, M_HEADS, M_DK, M_DK), F32), jnp.zeros((1, bp, M_HEADS, M_DK), F32),
                  jnp.zeros((1, bp, 1, LANES), F32), jnp.zeros((1, bp, CONV_W - 1, S_WIDTH), F32))
    m_pad = jnp.pad(state_m, ((0, 0), (0, 0), (0, LANES - M_HEADS))).reshape(DEPTH, bs, 1, LANES)

    tm_p = 512
    nb_s = 16
    xp, xs = x_prompt, x_sample
    outs = [[] for _ in range(8)]
    for l in range(DEPTH):
        last = l == DEPTH - 1
        xp, c1, n1, m1, cv1 = _mixer(xp, mod, bs, l, w, zero_state + (0,), nb=1, tm=tm_p, name=f"mixer_prompt_{l}")
        xs, c2, n2, m2, cv2 = _mixer(xs, mod, 0, l, w, (state_C, state_n, m_pad, state_conv, l),
                                     nb=nb_s, tm=ss, name=f"mixer_sample_{l}")
        for lst, val in zip(outs, (c1, n1, m1[:, 0, :M_HEADS], cv1, c2, n2, m2[:, 0, :M_HEADS], cv2)):
            lst.append(val)
        xp = _mlp(xp, mod, bs, l, w, gfin, nb=1, tm=tm_p, final=last, name=f"mlp_prompt_{l}")
        xs = _mlp(xs, mod, 0, l, w, gfin, nb=bs, tm=ss, final=last, name=f"mlp_sample_{l}")
    return (xp, xs) + tuple(jnp.stack(o) for o in outs)
```

```python
import functools

import jax
import jax.numpy as jnp
from jax import lax
from jax.experimental import pallas as pl
from jax.experimental.pallas import tpu as pltpu

F32 = jnp.float32
BF16 = jnp.bfloat16

D_MODEL = 1024
DEPTH = 4
M_HEADS = 4
M_DK = 128
M_WIDTH = M_HEADS * M_DK
S_WIDTH = 512
S_GROUPS = 4
CONV_W = 3
D_FF = 4 * D_MODEL
N_MOD = 6
EPS = 1e-6

REF_OFF_I = 4 * M_WIDTH
REF_OFF_F = REF_OFF_I + M_HEADS
REF_OFF_B = REF_OFF_F + M_HEADS
REF_N_IN = REF_OFF_B + 3 * S_WIDTH

LANES = 128
BLK = 128
OFF_Q, OFF_K, OFF_V, OFF_O = 0, 512, 1024, 1536
OFF_B, OFF_C, OFF_X = 2048, 2560, 3072
OFF_GI, OFF_GF = 3584, 3712
N_Z = 3840
FF_CHUNK = 1024
VMEM_LIMIT = 56 * 1024 * 1024


def _rms_rows(x):
    return x * lax.rsqrt(jnp.mean(x * x, axis=-1, keepdims=True) + EPS)


def _log_sigmoid(x):
    return jnp.minimum(x, 0.0) - jnp.log1p(jnp.exp(-jnp.abs(x)))


def _mod_kernel(c_ref, w_ref, b_ref, o_ref):
    c = c_ref[...]
    a = (c * jax.nn.sigmoid(c)).astype(BF16)
    o_ref[...] = jnp.dot(a, w_ref[...].astype(BF16), preferred_element_type=F32) + b_ref[...]


def _modulation(c_all, w_ada, b_ada):
    nrow = c_all.shape[0]
    b4 = b_ada.reshape(DEPTH, N_MOD, 1, D_MODEL)
    out = pl.pallas_call(
        _mod_kernel,
        out_shape=jax.ShapeDtypeStruct((DEPTH, N_MOD, nrow, D_MODEL), F32),
        grid=(DEPTH, N_MOD),
        in_specs=[
            pl.BlockSpec((nrow, D_MODEL), lambda l, j: (0, 0)),
            pl.BlockSpec((None, D_MODEL, D_MODEL), lambda l, j: (l, 0, j)),
            pl.BlockSpec((None, None, 1, D_MODEL), lambda l, j: (l, j, 0, 0)),
        ],
        out_specs=pl.BlockSpec((None, None, nrow, D_MODEL), lambda l, j: (l, j, 0, 0)),
        compiler_params=pltpu.CompilerParams(dimension_semantics=("arbitrary", "arbitrary")),
        name="adaln_modulation",
    )(c_all, w_ada, b4)
    return out.reshape(DEPTH, N_MOD, nrow, 1, D_MODEL)


def _mixer_kernel(x_ref, sh_ref, sc_ref, gt_ref, g1_ref, win_ref, bin_ref, cw_ref, gmix_ref, wout_ref,
                  c0_ref, n0_ref, m0_ref, cv0_ref,
                  xo_ref, c_ref, n_ref, m_ref, cv_ref,
                  q_s, k_s, v_s, og_s, mix_s, ubuf_s, p_s, g_s, we_s, em_s, rt_s,
                  *, nb, tm, seg, loop_blocks, single_tile):
    rows = nb * tm
    nblk = rows // BLK
    nseg = BLK // seg

    if single_tile:
        c_in, n_in, m_in, cv_in = c0_ref, n0_ref, m0_ref, cv0_ref
    else:
        c_in, n_in, m_in, cv_in = c_ref, n_ref, m_ref, cv_ref

        @pl.when(pl.program_id(1) == 0)
        def _():
            c_ref[...] = c0_ref[...]
            n_ref[...] = n0_ref[...]
            m_ref[...] = m0_ref[...]
            cv_ref[...] = cv0_ref[...]

    x3 = x_ref[...]
    h3 = _rms_rows(x3) * (g1_ref[...] * (1.0 + sc_ref[...])) + sh_ref[...]
    hb = h3.reshape(rows, D_MODEL).astype(BF16)

    def proj(off, width):
        return (jnp.dot(hb, win_ref[:, off:off + width], preferred_element_type=F32)
                + bin_ref[:, off:off + width])

    gates = proj(OFF_GI, 2 * LANES)
    q_s[...] = proj(OFF_Q, M_WIDTH).astype(BF16)

    logi = gates[:, :LANES]
    logf = _log_sigmoid(gates[:, LANES:])
    rin = lax.broadcasted_iota(jnp.int32, (rows, LANES), 0) & (tm - 1)
    bt = logf
    shift = 1
    while shift < tm:
        bt = bt + jnp.where(rin >= shift, pltpu.roll(bt, shift, axis=0), 0.0)
        shift *= 2
    r = logi - bt
    cm = r
    shift = 1
    while shift < tm:
        cm = jnp.maximum(cm, jnp.where(rin >= shift, pltpu.roll(cm, shift, axis=0), -jnp.inf))
        shift *= 2
    m0b = jnp.broadcast_to(m_in[...], (nb, tm, LANES)).reshape(rows, LANES)
    cmx = jnp.maximum(m0b, cm)
    p = -cmx
    m = bt + cmx
    m_ref[...] = m.reshape(nb, tm, LANES)[:, tm - 1:tm, :]
    nchunk = rows // seg
    p3 = p.reshape(nchunk, seg, LANES)
    p_before = jnp.where(rin == 0, -m0b, pltpu.roll(p, 1, axis=0)).reshape(nchunk, seg, LANES)[:, 0:1, :]
    p_s[...] = p
    g_s[...] = jnp.exp(p3 - p_before).reshape(rows, LANES)
    we_s[...] = jnp.exp(p3[:, seg - 1:seg, :] + r.reshape(nchunk, seg, LANES)).reshape(rows, LANES)
    em_s[...] = jnp.exp(-m)
    for bi in range(nblk):
        rt_s[:, bi * BLK:(bi + 1) * BLK] = r[bi * BLK:(bi + 1) * BLK, :].T

    k_s[...] = (proj(OFF_K, M_WIDTH) * (M_DK ** -0.5)).astype(BF16)
    v_s[...] = proj(OFF_V, M_WIDTH).astype(BF16)
    og_s[...] = jax.nn.sigmoid(proj(OFF_O, M_WIDTH))

    ri = lax.broadcasted_iota(jnp.int32, (BLK, BLK), 0)
    ci = lax.broadcasted_iota(jnp.int32, (BLK, BLK), 1)
    mask = ci <= ri
    if nseg > 1:
        sh = seg.bit_length() - 1
        mask = mask & ((ri >> sh) == (ci >> sh))

    def block_body(bi):
        r0 = bi * BLK
        if not isinstance(bi, int):
            r0 = pl.multiple_of(r0, BLK)
        rsl = pl.ds(r0, BLK)
        for h in range(M_HEADS):
            hs = slice(h * M_DK, (h + 1) * M_DK)
            qh = q_s[rsl, hs]
            kh = k_s[rsl, hs]
            vh = v_s[rsl, hs]
            s = lax.dot_general(qh, kh, (((1,), (1,)), ((), ())), preferred_element_type=F32)
            logw = p_s[rsl, h:h + 1] + rt_s[h:h + 1, pl.ds(r0, BLK)]
            sw = s * jnp.exp(jnp.where(mask, logw, -jnp.inf))
            rowsum = jnp.sum(sw, axis=-1, keepdims=True)
            intra = jnp.dot(sw.astype(BF16), vh, preferred_element_type=F32)
            gc = g_s[rsl, h:h + 1]
            ec = em_s[rsl, h:h + 1]
            kw = kh.astype(F32) * we_s[rsl, h:h + 1]
            qf = qh.astype(F32)
            for j in range(nseg):
                if tm >= BLK:
                    seq = r0 // tm
                else:
                    seq = bi * (BLK // tm) + j
                sl = slice(j * seg, (j + 1) * seg)
                rj = r0 + j * seg
                if not isinstance(bi, int):
                    rj = pl.multiple_of(rj, seg)
                c0 = c_in[seq, h]
                n0 = n_in[seq, h:h + 1, :]
                inter = jnp.dot(qh[sl], c0.astype(BF16), preferred_element_type=F32)
                qn = jnp.sum(qf[sl] * n0, axis=-1, keepdims=True)
                num = gc[sl] * inter + intra[sl]
                den = gc[sl] * qn + rowsum[sl]
                hh = num * (1.0 / jnp.maximum(jnp.abs(den), ec[sl]))
                out = _rms_rows(hh) * og_s[pl.ds(rj, seg), hs] * gmix_ref[:, hs]
                mix_s[pl.ds(rj, seg), hs] = out.astype(BF16)
                decay = gc[sl][seg - 1:seg, :]
                c_ref[seq, h] = decay * c0 + lax.dot_general(
                    kw[sl].astype(BF16), vh[sl], (((0,), (0,)), ((), ())), preferred_element_type=F32)
                n_ref[seq, h:h + 1, :] = decay * n0 + jnp.sum(kw[sl], axis=0, keepdims=True)

    def conv_inputs():
        ubuf_s[:, 6:8, :] = cv_in[...]
        u = proj(OFF_C, S_WIDTH) * proj(OFF_X, S_WIDTH)
        ubuf_s[:, 8:8 + tm, :] = u.reshape(nb, tm, S_WIDTH)
        cv_ref[...] = ubuf_s[:, tm + 6:tm + 8, :]

    def conv_outputs():
        bg = proj(OFF_B, S_WIDTH)
        cw = cw_ref[...]
        yc = (cw[0:1] * ubuf_s[:, 6:6 + tm, :] + cw[1:2] * ubuf_s[:, 7:7 + tm, :]
              + cw[2:3] * ubuf_s[:, 8:8 + tm, :])
        ysv = (bg.reshape(nb, tm, S_WIDTH) * yc).reshape(rows, S_WIDTH)
        for grp in range(S_GROUPS):
            lo = grp * LANES
            blk = _rms_rows(ysv[:, lo:lo + LANES])
            mix_s[:, M_WIDTH + lo:M_WIDTH + lo + LANES] = (
                blk * gmix_ref[:, M_WIDTH + lo:M_WIDTH + lo + LANES]).astype(BF16)

    def conv_half_out():
        return jnp.dot(mix_s[:, M_WIDTH:], wout_ref[M_WIDTH:, :], preferred_element_type=F32)

    if loop_blocks:
        conv_inputs()
        conv_outputs()
        y_conv = conv_half_out()

        def fbody(bi, carry):
            block_body(bi)
            return carry
        lax.fori_loop(0, nblk, fbody, 0)
    else:
        side = [conv_inputs, conv_outputs, conv_half_out]
        y_conv = None
        for bi in range(nblk):
            block_body(bi)
            if bi < len(side):
                y_conv = side[bi]()
        for fn in side[nblk:]:
            y_conv = fn()

    y = y_conv + jnp.dot(mix_s[:, :M_WIDTH], wout_ref[:M_WIDTH, :], preferred_element_type=F32)
    xo_ref[...] = x3 + gt_ref[...] * y.reshape(nb, tm, D_MODEL)


def _mixer(x, mod, mod_row0, layer, w, state, *, nb, tm, name):
    bsz, t_len, _ = x.shape
    rows = nb * tm
    seg = min(BLK, tm)
    assert rows % BLK == 0 and bsz % nb == 0 and t_len % tm == 0 and mod_row0 % nb == 0
    assert tm & (tm - 1) == 0 and (tm >= BLK or nb * tm >= BLK)
    c0, n0, m0, cv0, st_layer = state
    grid = (bsz // nb, t_len // tm)
    mrow = mod_row0 // nb

    def mod_spec(j):
        return pl.BlockSpec((None, None, nb, 1, D_MODEL), lambda b, t: (layer, j, mrow + b, 0, 0))

    def wspec(shape):
        nd = len(shape)
        return pl.BlockSpec((None,) + shape, lambda b, t: (layer,) + (0,) * nd)

    in_specs = [
        pl.BlockSpec((nb, tm, D_MODEL), lambda b, t: (b, t, 0)),
        mod_spec(0), mod_spec(1), mod_spec(2),
        wspec((1, D_MODEL)),
        wspec((D_MODEL, N_Z)),
        wspec((1, N_Z)),
        wspec((CONV_W, S_WIDTH)),
        wspec((1, D_MODEL)),
        wspec((D_MODEL, D_MODEL)),
        pl.BlockSpec((None, nb, M_HEADS, M_DK, M_DK), lambda b, t: (st_layer, b, 0, 0, 0)),
        pl.BlockSpec((None, nb, M_HEADS, M_DK), lambda b, t: (st_layer, b, 0, 0)),
        pl.BlockSpec((None, nb, 1, LANES), lambda b, t: (st_layer, b, 0, 0)),
        pl.BlockSpec((None, nb, CONV_W - 1, S_WIDTH), lambda b, t: (st_layer, b, 0, 0)),
    ]
    out_shape = (
        jax.ShapeDtypeStruct(x.shape, F32),
        jax.ShapeDtypeStruct((bsz, M_HEADS, M_DK, M_DK), F32),
        jax.ShapeDtypeStruct((bsz, M_HEADS, M_DK), F32),
        jax.ShapeDtypeStruct((bsz, 1, LANES), F32),
        jax.ShapeDtypeStruct((bsz, CONV_W - 1, S_WIDTH), F32),
    )
    out_specs = (
        pl.BlockSpec((nb, tm, D_MODEL), lambda b, t: (b, t, 0)),
        pl.BlockSpec((nb, M_HEADS, M_DK, M_DK), lambda b, t: (b, 0, 0, 0)),
        pl.BlockSpec((nb, M_HEADS, M_DK), lambda b, t: (b, 0, 0)),
        pl.BlockSpec((nb, 1, LANES), lambda b, t: (b, 0, 0)),
        pl.BlockSpec((nb, CONV_W - 1, S_WIDTH), lambda b, t: (b, 0, 0)),
    )
    scratch = [
        pltpu.VMEM((rows, M_WIDTH), BF16),
        pltpu.VMEM((rows, M_WIDTH), BF16),
        pltpu.VMEM((rows, M_WIDTH), BF16),
        pltpu.VMEM((rows, M_WIDTH), F32),
        pltpu.VMEM((rows, D_MODEL), BF16),
        pltpu.VMEM((nb, tm + 8, S_WIDTH), F32),
        pltpu.VMEM((rows, LANES), F32),
        pltpu.VMEM((rows, LANES), F32),
        pltpu.VMEM((rows, LANES), F32),
        pltpu.VMEM((rows, LANES), F32),
        pltpu.VMEM((LANES, rows), F32),
    ]
    kern = functools.partial(_mixer_kernel, nb=nb, tm=tm, seg=seg, loop_blocks=(tm < BLK),
                             single_tile=(t_len == tm))
    return pl.pallas_call(
        kern, out_shape=out_shape, grid=grid, in_specs=in_specs, out_specs=out_specs,
        scratch_shapes=scratch,
        compiler_params=pltpu.CompilerParams(dimension_semantics=("arbitrary", "arbitrary"),
                                             vmem_limit_bytes=VMEM_LIMIT),
        name=name,
    )(x, mod, mod, mod, w["g1"], w["w_in"], w["b_in"], w["conv_w"], w["g_mix"], w["w_out"],
      c0, n0, m0, cv0)


def _mlp_kernel(x_ref, sh_ref, sc_ref, gt_ref, g2_ref, wup_ref, wdn_ref, gf_ref, xo_ref, *, nb, tm, final):
    rows = nb * tm
    x3 = x_ref[...]
    h3 = _rms_rows(x3) * (g2_ref[...] * (1.0 + sc_ref[...])) + sh_ref[...]
    hb = h3.reshape(rows, D_MODEL).astype(BF16)
    acc = jnp.zeros((rows, D_MODEL), F32)
    for c in range(D_FF // FF_CHUNK):
        lo = c * FF_CHUNK
        up = jnp.dot(hb, wup_ref[:, lo:lo + FF_CHUNK], preferred_element_type=F32)
        act = jnp.square(jnp.maximum(up, 0.0)).astype(BF16)
        acc = acc + jnp.dot(act, wdn_ref[lo:lo + FF_CHUNK, :], preferred_element_type=F32)
    xn = x3 + gt_ref[...] * acc.reshape(nb, tm, D_MODEL)
    if final:
        xn = _rms_rows(xn) * gf_ref[...]
    xo_ref[...] = xn


def _mlp(x, mod, mod_row0, layer, w, g_final, *, nb, tm, final, name):
    bsz, t_len, _ = x.shape
    assert bsz % nb == 0 and t_len % tm == 0 and mod_row0 % nb == 0
    grid = (bsz // nb, t_len // tm)
    mrow = mod_row0 // nb

    def mod_spec(j):
        return pl.BlockSpec((None, None, nb, 1, D_MODEL), lambda b, t: (layer, j, mrow + b, 0, 0))

    in_specs = [
        pl.BlockSpec((nb, tm, D_MODEL), lambda b, t: (b, t, 0)),
        mod_spec(3), mod_spec(4), mod_spec(5),
        pl.BlockSpec((None, 1, D_MODEL), lambda b, t: (layer, 0, 0)),
        pl.BlockSpec((None, D_MODEL, D_FF), lambda b, t: (layer, 0, 0)),
        pl.BlockSpec((None, D_FF, D_MODEL), lambda b, t: (layer, 0, 0)),
        pl.BlockSpec((1, D_MODEL), lambda b, t: (0, 0)),
    ]
    kern = functools.partial(_mlp_kernel, nb=nb, tm=tm, final=final)
    return pl.pallas_call(
        kern, out_shape=jax.ShapeDtypeStruct(x.shape, F32), grid=grid, in_specs=in_specs,
        out_specs=pl.BlockSpec((nb, tm, D_MODEL), lambda b, t: (b, t, 0)),
        compiler_params=pltpu.CompilerParams(dimension_semantics=("arbitrary", "arbitrary"),
                                             vmem_limit_bytes=VMEM_LIMIT),
        name=name,
    )(x, mod, mod, mod, w["g2"], w["w_up"], w["w_down"], g_final)


def kernel(x_prompt, x_sample, c_prompt, c_sample, state_C, state_n, state_m, state_conv, w_ada, b_ada,
           g_norm1, w_in, b_in, conv_w, g_mix_out, w_out, g_norm2, w_up, w_down, g_final):
    bp, sp, _ = x_prompt.shape
    bs, ss, _ = x_sample.shape

    def relayout(a):
        gi = a[..., REF_OFF_I:REF_OFF_F]
        gf = a[..., REF_OFF_F:REF_OFF_B]
        pad = [(0, 0)] * (a.ndim - 1) + [(0, LANES - M_HEADS)]
        return jnp.concatenate([a[..., :REF_OFF_I], a[..., REF_OFF_B:], jnp.pad(gi, pad), jnp.pad(gf, pad)], axis=-1)

    w = {
        "g1": g_norm1.reshape(DEPTH, 1, D_MODEL),
        "w_in": relayout(w_in).astype(BF16),
        "b_in": relayout(b_in).reshape(DEPTH, 1, N_Z),
        "conv_w": conv_w,
        "g_mix": g_mix_out.reshape(DEPTH, 1, D_MODEL),
        "w_out": w_out.astype(BF16),
        "g2": g_norm2.reshape(DEPTH, 1, D_MODEL),
        "w_up": w_up.astype(BF16),
        "w_down": w_down.astype(BF16),
    }
    gfin = g_final.reshape(1, D_MODEL)

    mod = _modulation(jnp.concatenate([c_sample, c_prompt], axis=0), w_ada, b_ada)

    zero_state = (jnp.zeros((1, bp, M_HEADS, M_DK, M_DK), F32), jnp.zeros((1, bp, M_HEADS, M_DK), F32),
                  jnp.zeros((1, bp, 1, LANES), F32), jnp.zeros((1, bp, CONV_W - 1, S_WIDTH), F32))
    m_pad = jnp.pad(state_m, ((0, 0), (0, 0), (0, LANES - M_HEADS))).reshape(DEPTH, bs, 1, LANES)

    tm_p = 512
    nb_s = 16
    xp, xs = x_prompt, x_sample
    outs = [[] for _ in range(8)]
    for l in range(DEPTH):
        last = l == DEPTH - 1
        xp, c1, n1, m1, cv1 = _mixer(xp, mod, bs, l, w, zero_state + (0,), nb=1, tm=tm_p, name=f"mixer_prompt_{l}")
        xs, c2, n2, m2, cv2 = _mixer(xs, mod, 0, l, w, (state_C, state_n, m_pad, state_conv, l),
                                     nb=nb_s, tm=ss, name=f"mixer_sample_{l}")
        for lst, val in zip(outs, (c1, n1, m1[:, 0, :M_HEADS], cv1, c2, n2, m2[:, 0, :M_HEADS], cv2)):
            lst.append(val)
        xp = _mlp(xp, mod, bs, l, w, gfin, nb=1, tm=tm_p, final=last, name=f"mlp_prompt_{l}")
        xs = _mlp(xs, mod, 0, l, w, gfin, nb=bs, tm=ss, final=last, name=f"mlp_sample_{l}")
    return (xp, xs) + tuple(jnp.stack(o) for o in outs)
```

```python
import functools
import types

import jax
import jax.numpy as jnp
from jax import lax
from jax.experimental import pallas as pl
from jax.experimental.pallas import tpu as pltpu

F32 = jnp.float32
BF16 = jnp.bfloat16

D_MODEL = 1024
DEPTH = 4
M_HEADS = 4
M_DK = 128
M_WIDTH = M_HEADS * M_DK
S_WIDTH = 512
S_GROUPS = 4
CONV_W = 3
D_FF = 4 * D_MODEL
N_MOD = 6
EPS = 1e-6

REF_OFF_I = 4 * M_WIDTH
REF_OFF_F = REF_OFF_I + M_HEADS
REF_OFF_B = REF_OFF_F + M_HEADS
REF_N_IN = REF_OFF_B + 3 * S_WIDTH

LANES = 128
BLK = 128
OFF_Q, OFF_K, OFF_V, OFF_O = 0, 512, 1024, 1536
OFF_B, OFF_C, OFF_X = 2048, 2560, 3072
OFF_GI, OFF_GF = 3584, 3712
N_Z = 3840
FF_CHUNK = 1024
RELAYOUT_ROWS = 256
VMEM_LIMIT = 56 * 1024 * 1024


def _rms_rows(x):
    return x * lax.rsqrt(jnp.mean(x * x, axis=-1, keepdims=True) + EPS)


def _log_sigmoid(x):
    return jnp.minimum(x, 0.0) - jnp.log1p(jnp.exp(-jnp.abs(x)))


def _mod_kernel(c_ref, w_ref, b_ref, o_ref):
    c = c_ref[...]
    a = (c * jax.nn.sigmoid(c)).astype(BF16)
    o_ref[...] = jnp.dot(a, w_ref[...].astype(BF16), preferred_element_type=F32) + b_ref[...]


def _modulation(c_all, w_ada, b_ada):
    nrow = c_all.shape[0]
    b4 = b_ada.reshape(DEPTH, N_MOD, 1, D_MODEL)
    out = pl.pallas_call(
        _mod_kernel,
        out_shape=jax.ShapeDtypeStruct((DEPTH, N_MOD, nrow, D_MODEL), F32),
        grid=(DEPTH, N_MOD),
        in_specs=[
            pl.BlockSpec((nrow, D_MODEL), lambda l, j: (0, 0)),
            pl.BlockSpec((None, D_MODEL, D_MODEL), lambda l, j: (l, 0, j)),
            pl.BlockSpec((None, None, 1, D_MODEL), lambda l, j: (l, j, 0, 0)),
        ],
        out_specs=pl.BlockSpec((None, None, nrow, D_MODEL), lambda l, j: (l, j, 0, 0)),
        compiler_params=pltpu.CompilerParams(dimension_semantics=("arbitrary", "arbitrary")),
        name="adaln_modulation",
    )(c_all, w_ada, b4)
    return out.reshape(DEPTH, N_MOD, nrow, 1, D_MODEL)


def _relayout_kernel(w_ref, o_ref):
    o_ref[:, :REF_OFF_I] = w_ref[:, :REF_OFF_I].astype(BF16)
    o_ref[:, OFF_B:OFF_GI] = w_ref[:, REF_OFF_B:REF_N_IN].astype(BF16)
    g = w_ref[:, REF_OFF_I:REF_OFF_I + LANES]
    lane = lax.broadcasted_iota(jnp.int32, g.shape, 1)
    o_ref[:, OFF_GI:OFF_GF] = jnp.where(lane < M_HEADS, g, 0.0).astype(BF16)
    o_ref[:, OFF_GF:N_Z] = jnp.where(lane < M_HEADS, pltpu.roll(g, LANES - M_HEADS, axis=1), 0.0).astype(BF16)


def _relayout_w_in(w_in):
    return pl.pallas_call(
        _relayout_kernel,
        out_shape=jax.ShapeDtypeStruct((DEPTH, D_MODEL, N_Z), BF16),
        grid=(DEPTH, D_MODEL // RELAYOUT_ROWS),
        in_specs=[pl.BlockSpec((None, RELAYOUT_ROWS, REF_N_IN), lambda l, i: (l, i, 0))],
        out_specs=pl.BlockSpec((None, RELAYOUT_ROWS, N_Z), lambda l, i: (l, i, 0)),
        compiler_params=pltpu.CompilerParams(dimension_semantics=("arbitrary", "arbitrary")),
        name="relayout_w_in",
    )(w_in)


def _mixer_steps(r, *, nb, tm, seg, first_tile):
    rows = nb * tm
    nblk = rows // BLK
    nseg = BLK // seg
    v = types.SimpleNamespace()

    if first_tile is None:
        c_in, n_in, m_in, cv_in = r.c0, r.n0, r.m0, r.cv0
    else:
        c_in, n_in, m_in, cv_in = r.c, r.n, r.m, r.cv

    def proj(off, width):
        return (jnp.dot(v.hb, r.win[:, off:off + width], preferred_element_type=F32)
                + r.bin[:, off:off + width])

    def norm_gates():
        if first_tile is not None:
            @pl.when(first_tile)
            def _():
                r.c[...] = r.c0[...]
                r.n[...] = r.n0[...]
                r.m[...] = r.m0[...]
                r.cv[...] = r.cv0[...]

        v.x3 = r.x[...]
        h3 = _rms_rows(v.x3) * (r.g1[...] * (1.0 + r.sc[...])) + r.sh[...]
        v.hb = h3.reshape(rows, D_MODEL).astype(BF16)
        v.gates = proj(OFF_GI, 2 * LANES)

    def q_proj():
        r.q_s[...] = proj(OFF_Q, M_WIDTH).astype(BF16)

    def gate_scalars():
        logi = v.gates[:, :LANES]
        logf = _log_sigmoid(v.gates[:, LANES:])
        rin = lax.broadcasted_iota(jnp.int32, (rows, LANES), 0) & (tm - 1)
        bt = logf
        shift = 1
        while shift < tm:
            bt = bt + jnp.where(rin >= shift, pltpu.roll(bt, shift, axis=0), 0.0)
            shift *= 2
        rr = logi - bt
        cm = rr
        shift = 1
        while shift < tm:
            cm = jnp.maximum(cm, jnp.where(rin >= shift, pltpu.roll(cm, shift, axis=0), -jnp.inf))
            shift *= 2
        m0b = jnp.broadcast_to(m_in[...], (nb, tm, LANES)).reshape(rows, LANES)
        cmx = jnp.maximum(m0b, cm)
        p = -cmx
        m = bt + cmx
        r.m[...] = m.reshape(nb, tm, LANES)[:, tm - 1:tm, :]
        nchunk = rows // seg
        p3 = p.reshape(nchunk, seg, LANES)
        p_before = jnp.where(rin == 0, -m0b, pltpu.roll(p, 1, axis=0)).reshape(nchunk, seg, LANES)[:, 0:1, :]
        r.p_s[...] = p
        r.g_s[...] = jnp.exp(p3 - p_before).reshape(rows, LANES)
        r.we_s[...] = jnp.exp(p3[:, seg - 1:seg, :] + rr.reshape(nchunk, seg, LANES)).reshape(rows, LANES)
        r.em_s[...] = jnp.exp(-m)
        for bi in range(nblk):
            r.rt_s[:, bi * BLK:(bi + 1) * BLK] = rr[bi * BLK:(bi + 1) * BLK, :].T

    def k_proj():
        r.k_s[...] = (proj(OFF_K, M_WIDTH) * (M_DK ** -0.5)).astype(BF16)

    def v_proj():
        r.v_s[...] = proj(OFF_V, M_WIDTH).astype(BF16)

    def o_proj():
        r.og_s[...] = jax.nn.sigmoid(proj(OFF_O, M_WIDTH))

    def block_diag(a, rowseg):
        if nseg == 1:
            return a
        zero = jnp.zeros_like(a)
        return jnp.concatenate([jnp.where(rowseg == j, a, zero) for j in range(nseg)], axis=1)

    def block_body(bi):
        ri = lax.broadcasted_iota(jnp.int32, (BLK, BLK), 0)
        ci = lax.broadcasted_iota(jnp.int32, (BLK, BLK), 1)
        sh = seg.bit_length() - 1
        rowseg = ri >> sh
        mask = ci <= ri
        if nseg > 1:
            mask = mask & (rowseg == (ci >> sh))
        r0 = bi * BLK
        rsl = slice(r0, r0 + BLK)
        seq0 = r0 // tm
        ssl = slice(seq0, seq0 + nseg)
        g_blk = r.g_s[rsl, :].reshape(nseg, seg, LANES)
        for h in range(M_HEADS):
            hs = slice(h * M_DK, (h + 1) * M_DK)
            qh = r.q_s[rsl, hs]
            kh = r.k_s[rsl, hs]
            vh = r.v_s[rsl, hs]
            s = lax.dot_general(qh, kh, (((1,), (1,)), ((), ())), preferred_element_type=F32)
            logw = r.p_s[rsl, h:h + 1] + r.rt_s[h:h + 1, rsl]
            sw = s * jnp.exp(jnp.where(mask, logw, -jnp.inf))
            rowsum = jnp.sum(sw, axis=-1, keepdims=True)
            intra = jnp.dot(sw.astype(BF16), vh, preferred_element_type=F32)
            gc = r.g_s[rsl, h:h + 1]
            ec = r.em_s[rsl, h:h + 1]
            kw = kh.astype(F32) * r.we_s[rsl, h:h + 1]
            c0 = c_in[ssl, h]
            n0 = n_in[ssl, h:h + 1, :]
            inter = jnp.dot(block_diag(qh, rowseg), c0.astype(BF16).reshape(nseg * M_DK, M_DK),
                            preferred_element_type=F32)
            n_rows = jnp.broadcast_to(n0, (nseg, seg, M_DK)).reshape(BLK, M_DK)
            qn = jnp.sum(qh.astype(F32) * n_rows, axis=-1, keepdims=True)
            num = gc * inter + intra
            den = gc * qn + rowsum
            hh = num * (1.0 / jnp.maximum(jnp.abs(den), ec))
            r.hn_s[rsl, hs] = _rms_rows(hh)
            upd = lax.dot_general(block_diag(kw.astype(BF16), rowseg), vh, (((0,), (0,)), ((), ())),
                                  preferred_element_type=F32)
            decay = g_blk[:, seg - 1:seg, h:h + 1]
            r.c[ssl, h] = decay * c0 + upd.reshape(nseg, M_DK, M_DK)
            r.n[ssl, h:h + 1, :] = decay * n0 + jnp.sum(kw.reshape(nseg, seg, M_DK), axis=1, keepdims=True)

    def conv_inputs():
        r.ubuf_s[:, 6:8, :] = cv_in[...]
        u = proj(OFF_C, S_WIDTH) * proj(OFF_X, S_WIDTH)
        r.ubuf_s[:, 8:8 + tm, :] = u.reshape(nb, tm, S_WIDTH)
        r.cv[...] = r.ubuf_s[:, tm + 6:tm + 8, :]

    def conv_outputs():
        bg = proj(OFF_B, S_WIDTH)
        cw = r.cw[...]
        yc = (cw[0:1] * r.ubuf_s[:, 6:6 + tm, :] + cw[1:2] * r.ubuf_s[:, 7:7 + tm, :]
              + cw[2:3] * r.ubuf_s[:, 8:8 + tm, :])
        ysv = (bg.reshape(nb, tm, S_WIDTH) * yc).reshape(rows, S_WIDTH)
        for grp in range(S_GROUPS):
            lo = grp * LANES
            blk = _rms_rows(ysv[:, lo:lo + LANES])
            r.mix_s[:, M_WIDTH + lo:M_WIDTH + lo + LANES] = (
                blk * r.gmix[:, M_WIDTH + lo:M_WIDTH + lo + LANES]).astype(BF16)

    def conv_half_out():
        v.y_conv = jnp.dot(r.mix_s[:, M_WIDTH:], r.wout[M_WIDTH:, :], preferred_element_type=F32)

    def finish():
        r.mix_s[:, :M_WIDTH] = (r.hn_s[...] * r.og_s[...] * r.gmix[:, :M_WIDTH]).astype(BF16)
        y = v.y_conv + jnp.dot(r.mix_s[:, :M_WIDTH], r.wout[:M_WIDTH, :], preferred_element_type=F32)
        r.xo[...] = v.x3 + r.gt[...] * y.reshape(nb, tm, D_MODEL)

    fillers = [o_proj, conv_inputs, conv_outputs, conv_half_out]
    steps = [norm_gates, q_proj, gate_scalars, k_proj, v_proj]
    for bi in range(nblk):
        steps.append(functools.partial(block_body, bi))
        if fillers:
            steps.append(fillers.pop(0))
    return steps + fillers + [finish]


_MIXER_REFS = (
    "x", "sh", "sc", "gt", "g1", "win", "bin", "cw", "gmix", "wout", "c0", "n0", "m0", "cv0",
    "xo", "c", "n", "m", "cv",
    "q_s", "k_s", "v_s", "og_s", "hn_s", "mix_s", "ubuf_s", "p_s", "g_s", "we_s", "em_s", "rt_s",
)


def _mixer_kernel(*refs, nb, tm, seg, nt):
    r = types.SimpleNamespace(**dict(zip(_MIXER_REFS, refs, strict=True)))
    first_tile = None if nt == 1 else pl.program_id(1) == 0
    for step in _mixer_steps(r, nb=nb, tm=tm, seg=seg, first_tile=first_tile):
        step()


def _mixer(x, mod, mod_row0, layer, w, state, *, nb, tm, name):
    bsz, t_len, _ = x.shape
    rows = nb * tm
    seg = min(BLK, tm)
    assert rows % BLK == 0 and bsz % nb == 0 and t_len % tm == 0 and mod_row0 % nb == 0
    assert tm & (tm - 1) == 0 and (nb == 1 or tm == t_len)
    c0, n0, m0, cv0, st_layer = state
    grid = (bsz // nb, t_len // tm)
    mrow = mod_row0 // nb

    def mod_spec(j):
        return pl.BlockSpec((None, None, nb, 1, D_MODEL), lambda b, t: (layer, j, mrow + b, 0, 0))

    def wspec(shape):
        nd = len(shape)
        return pl.BlockSpec((None,) + shape, lambda b, t: (layer,) + (0,) * nd)

    def sspec(shape):
        nd = len(shape)
        return pl.BlockSpec((None, nb) + shape, lambda b, t: (st_layer, b) + (0,) * nd)

    def ospec(shape):
        nd = len(shape)
        return pl.BlockSpec((nb,) + shape, lambda b, t: (b,) + (0,) * nd)

    state_shapes = ((M_HEADS, M_DK, M_DK), (M_HEADS, M_DK), (1, LANES), (CONV_W - 1, S_WIDTH))
    in_specs = [
        pl.BlockSpec((nb, tm, D_MODEL), lambda b, t: (b, t, 0)),
        mod_spec(0), mod_spec(1), mod_spec(2),
        wspec((1, D_MODEL)),
        wspec((D_MODEL, N_Z)),
        wspec((1, N_Z)),
        wspec((CONV_W, S_WIDTH)),
        wspec((1, D_MODEL)),
        wspec((D_MODEL, D_MODEL)),
    ] + [sspec(s) for s in state_shapes]
    out_shape = (jax.ShapeDtypeStruct(x.shape, F32),) + tuple(
        jax.ShapeDtypeStruct((bsz,) + s, F32) for s in state_shapes)
    out_specs = (pl.BlockSpec((nb, tm, D_MODEL), lambda b, t: (b, t, 0)),) + tuple(ospec(s) for s in state_shapes)
    scratch = [
        pltpu.VMEM((rows, M_WIDTH), BF16),
        pltpu.VMEM((rows, M_WIDTH), BF16),
        pltpu.VMEM((rows, M_WIDTH), BF16),
        pltpu.VMEM((rows, M_WIDTH), F32),
        pltpu.VMEM((rows, M_WIDTH), F32),
        pltpu.VMEM((rows, D_MODEL), BF16),
        pltpu.VMEM((nb, tm + 8, S_WIDTH), F32),
        pltpu.VMEM((rows, LANES), F32),
        pltpu.VMEM((rows, LANES), F32),
        pltpu.VMEM((rows, LANES), F32),
        pltpu.VMEM((rows, LANES), F32),
        pltpu.VMEM((LANES, rows), F32),
    ]
    kern = functools.partial(_mixer_kernel, nb=nb, tm=tm, seg=seg, nt=t_len // tm)
    return pl.pallas_call(
        kern, out_shape=out_shape, grid=grid, in_specs=in_specs, out_specs=out_specs,
        scratch_shapes=scratch,
        compiler_params=pltpu.CompilerParams(dimension_semantics=("arbitrary", "arbitrary"),
                                             vmem_limit_bytes=VMEM_LIMIT),
        name=name,
    )(x, mod, mod, mod, w["g1"], w["w_in"], w["b_in"], w["conv_w"], w["g_mix"], w["w_out"],
      c0, n0, m0, cv0)


def _mlp_kernel(x_ref, sh_ref, sc_ref, gt_ref, g2_ref, wup_ref, wdn_ref, gf_ref, xo_ref, *, nb, tm, final):
    rows = nb * tm
    x3 = x_ref[...]
    h3 = _rms_rows(x3) * (g2_ref[...] * (1.0 + sc_ref[...])) + sh_ref[...]
    hb = h3.reshape(rows, D_MODEL).astype(BF16)
    acc = jnp.zeros((rows, D_MODEL), F32)
    for c in range(D_FF // FF_CHUNK):
        lo = c * FF_CHUNK
        up = jnp.dot(hb, wup_ref[:, lo:lo + FF_CHUNK], preferred_element_type=F32)
        act = jnp.square(jnp.maximum(up, 0.0)).astype(BF16)
        acc = acc + jnp.dot(act, wdn_ref[lo:lo + FF_CHUNK, :], preferred_element_type=F32)
    xn = x3 + gt_ref[...] * acc.reshape(nb, tm, D_MODEL)
    if final:
        xn = _rms_rows(xn) * gf_ref[...]
    xo_ref[...] = xn


def _mlp(x, mod, mod_row0, layer, w, g_final, *, nb, tm, final, name):
    bsz, t_len, _ = x.shape
    assert bsz % nb == 0 and t_len % tm == 0 and mod_row0 % nb == 0
    grid = (bsz // nb, t_len // tm)
    mrow = mod_row0 // nb

    def mod_spec(j):
        return pl.BlockSpec((None, None, nb, 1, D_MODEL), lambda b, t: (layer, j, mrow + b, 0, 0))

    in_specs = [
        pl.BlockSpec((nb, tm, D_MODEL), lambda b, t: (b, t, 0)),
        mod_spec(3), mod_spec(4), mod_spec(5),
        pl.BlockSpec((None, 1, D_MODEL), lambda b, t: (layer, 0, 0)),
        pl.BlockSpec((None, D_MODEL, D_FF), lambda b, t: (layer, 0, 0)),
        pl.BlockSpec((None, D_FF, D_MODEL), lambda b, t: (layer, 0, 0)),
        pl.BlockSpec((1, D_MODEL), lambda b, t: (0, 0)),
    ]
    kern = functools.partial(_mlp_kernel, nb=nb, tm=tm, final=final)
    return pl.pallas_call(
        kern, out_shape=jax.ShapeDtypeStruct(x.shape, F32), grid=grid, in_specs=in_specs,
        out_specs=pl.BlockSpec((nb, tm, D_MODEL), lambda b, t: (b, t, 0)),
        compiler_params=pltpu.CompilerParams(dimension_semantics=("arbitrary", "arbitrary"),
                                             vmem_limit_bytes=VMEM_LIMIT),
        name=name,
    )(x, mod, mod, mod, w["g2"], w["w_up"], w["w_down"], g_final)


def kernel(x_prompt, x_sample, c_prompt, c_sample, state_C, state_n, state_m, state_conv, w_ada, b_ada,
           g_norm1, w_in, b_in, conv_w, g_mix_out, w_out, g_norm2, w_up, w_down, g_final):
    bp, sp, _ = x_prompt.shape
    bs, ss, _ = x_sample.shape

    def relayout_bias(a):
        gi = a[..., REF_OFF_I:REF_OFF_F]
        gf = a[..., REF_OFF_F:REF_OFF_B]
        pad = [(0, 0)] * (a.ndim - 1) + [(0, LANES - M_HEADS)]
        return jnp.concatenate([a[..., :REF_OFF_I], a[..., REF_OFF_B:], jnp.pad(gi, pad), jnp.pad(gf, pad)], axis=-1)

    w = {
        "g1": g_norm1.reshape(DEPTH, 1, D_MODEL),
        "w_in": _relayout_w_in(w_in),
        "b_in": relayout_bias(b_in).reshape(DEPTH, 1, N_Z),
        "conv_w": conv_w,
        "g_mix": g_mix_out.reshape(DEPTH, 1, D_MODEL),
        "w_out": w_out.astype(BF16),
        "g2": g_norm2.reshape(DEPTH, 1, D_MODEL),
        "w_up": w_up.astype(BF16),
        "w_down": w_down.astype(BF16),
    }
    gfin = g_final.reshape(1, D_MODEL)

    mod = _modulation(jnp.concatenate([c_sample, c_prompt], axis=0), w_ada, b_ada)

    zero_state = (jnp.zeros((1, bp, M_HEADS, M_DK, M_DK), F32), jnp.zeros((1, bp, M_HEADS, M_DK), F32),
                  jnp.zeros((1, bp, 1, LANES), F32), jnp.zeros((1, bp, CONV_W - 1, S_WIDTH), F32))
    m_pad = jnp.pad(state_m, ((0, 0), (0, 0), (0, LANES - M_HEADS))).reshape(DEPTH, bs, 1, LANES)

    tm_p = 512
    nb_s = 16
    xp, xs = x_prompt, x_sample
    outs = [[] for _ in range(8)]
    for l in range(DEPTH):
        last = l == DEPTH - 1
        xp, c1, n1, m1, cv1 = _mixer(xp, mod, bs, l, w, zero_state + (0,), nb=1, tm=tm_p, name=f"mixer_prompt_{l}")
        xs, c2, n2, m2, cv2 = _mixer(xs, mod, 0, l, w, (state_C, state_n, m_pad, state_conv, l),
                                     nb=nb_s, tm=ss, name=f"mixer_sample_{l}")
        for lst, val in zip(outs, (c1, n1, m1[:, 0, :M_HEADS], cv1, c2, n2, m2[:, 0, :M_HEADS], cv2)):
            lst.append(val)
        xp = _mlp(xp, mod, bs, l, w, gfin, nb=1, tm=tm_p, final=last, name=f"mlp_prompt_{l}")
        xs = _mlp(xs, mod, 0, l, w, gfin, nb=bs, tm=ss, final=last, name=f"mlp_sample_{l}")
    return (xp, xs) + tuple(jnp.stack(o) for o in outs)
```

```python
import functools
import types

import jax
import jax.numpy as jnp
from jax import lax
from jax.experimental import pallas as pl
from jax.experimental.pallas import tpu as pltpu

F32 = jnp.float32
BF16 = jnp.bfloat16

D_MODEL = 1024
DEPTH = 4
M_HEADS = 4
M_DK = 128
M_WIDTH = M_HEADS * M_DK
S_WIDTH = 512
S_GROUPS = 4
CONV_W = 3
D_FF = 4 * D_MODEL
N_MOD = 6
EPS = 1e-6

REF_OFF_I = 4 * M_WIDTH
REF_OFF_F = REF_OFF_I + M_HEADS
REF_OFF_B = REF_OFF_F + M_HEADS
REF_N_IN = REF_OFF_B + 3 * S_WIDTH

LANES = 128
BLK = 128
OFF_Q, OFF_K, OFF_V, OFF_O = 0, 512, 1024, 1536
OFF_B, OFF_C, OFF_X = 2048, 2560, 3072
OFF_GI, OFF_GF = 3584, 3712
N_Z = 3840
FF_CHUNK = 1024
RELAYOUT_ROWS = 256
VMEM_LIMIT = 56 * 1024 * 1024


def _rms_rows(x):
    return x * lax.rsqrt(jnp.mean(x * x, axis=-1, keepdims=True) + EPS)


def _log_sigmoid(x):
    return jnp.minimum(x, 0.0) - jnp.log1p(jnp.exp(-jnp.abs(x)))


def _mod_kernel(c_ref, w_ref, b_ref, o_ref):
    c = c_ref[...]
    a = (c * jax.nn.sigmoid(c)).astype(BF16)
    o_ref[...] = jnp.dot(a, w_ref[...].astype(BF16), preferred_element_type=F32) + b_ref[...]


def _modulation(c_all, w_ada, b_ada):
    nrow = c_all.shape[0]
    b4 = b_ada.reshape(DEPTH, N_MOD, 1, D_MODEL)
    out = pl.pallas_call(
        _mod_kernel,
        out_shape=jax.ShapeDtypeStruct((DEPTH, N_MOD, nrow, D_MODEL), F32),
        grid=(DEPTH, N_MOD),
        in_specs=[
            pl.BlockSpec((nrow, D_MODEL), lambda l, j: (0, 0)),
            pl.BlockSpec((None, D_MODEL, D_MODEL), lambda l, j: (l, 0, j)),
            pl.BlockSpec((None, None, 1, D_MODEL), lambda l, j: (l, j, 0, 0)),
        ],
        out_specs=pl.BlockSpec((None, None, nrow, D_MODEL), lambda l, j: (l, j, 0, 0)),
        compiler_params=pltpu.CompilerParams(dimension_semantics=("arbitrary", "arbitrary")),
        name="adaln_modulation",
    )(c_all, w_ada, b4)
    return out.reshape(DEPTH, N_MOD, nrow, 1, D_MODEL)


def _relayout_kernel(w_ref, o_ref):
    o_ref[:, :REF_OFF_I] = w_ref[:, :REF_OFF_I].astype(BF16)
    o_ref[:, OFF_B:OFF_GI] = w_ref[:, REF_OFF_B:REF_N_IN].astype(BF16)
    g = w_ref[:, REF_OFF_I:REF_OFF_I + LANES]
    lane = lax.broadcasted_iota(jnp.int32, g.shape, 1)
    o_ref[:, OFF_GI:OFF_GF] = jnp.where(lane < M_HEADS, g, 0.0).astype(BF16)
    o_ref[:, OFF_GF:N_Z] = jnp.where(lane < M_HEADS, pltpu.roll(g, LANES - M_HEADS, axis=1), 0.0).astype(BF16)


def _relayout_w_in(w_in):
    return pl.pallas_call(
        _relayout_kernel,
        out_shape=jax.ShapeDtypeStruct((DEPTH, D_MODEL, N_Z), BF16),
        grid=(DEPTH, D_MODEL // RELAYOUT_ROWS),
        in_specs=[pl.BlockSpec((None, RELAYOUT_ROWS, REF_N_IN), lambda l, i: (l, i, 0))],
        out_specs=pl.BlockSpec((None, RELAYOUT_ROWS, N_Z), lambda l, i: (l, i, 0)),
        compiler_params=pltpu.CompilerParams(dimension_semantics=("arbitrary", "arbitrary")),
        name="relayout_w_in",
    )(w_in)


def _mixer_steps(r, *, nb, tm, seg, first_tile):
    rows = nb * tm
    nblk = rows // BLK
    nseg = BLK // seg
    v = types.SimpleNamespace(blk={})

    if first_tile is None:
        c_in, n_in, m_in, cv_in = r.c0, r.n0, r.m0, r.cv0
    else:
        c_in, n_in, m_in, cv_in = r.c, r.n, r.m, r.cv

    def proj(off, width):
        return (jnp.dot(v.hb, r.win[:, off:off + width], preferred_element_type=F32)
                + r.bin[:, off:off + width])

    def norm_gates():
        if first_tile is not None:
            @pl.when(first_tile)
            def _():
                r.c[...] = r.c0[...]
                r.n[...] = r.n0[...]
                r.m[...] = r.m0[...]
                r.cv[...] = r.cv0[...]

        v.x3 = r.x[...]
        h3 = _rms_rows(v.x3) * (r.g1[...] * (1.0 + r.sc[...])) + r.sh[...]
        v.hb = h3.reshape(rows, D_MODEL).astype(BF16)
        v.gates = proj(OFF_GI, 2 * LANES)

    def q_proj():
        r.q_s[...] = proj(OFF_Q, M_WIDTH).astype(BF16)

    def gate_scalars():
        logi = v.gates[:, :LANES]
        logf = _log_sigmoid(v.gates[:, LANES:])
        rin = lax.broadcasted_iota(jnp.int32, (rows, LANES), 0) & (tm - 1)
        bt = logf
        shift = 1
        while shift < tm:
            bt = bt + jnp.where(rin >= shift, pltpu.roll(bt, shift, axis=0), 0.0)
            shift *= 2
        rr = logi - bt
        cm = rr
        shift = 1
        while shift < tm:
            cm = jnp.maximum(cm, jnp.where(rin >= shift, pltpu.roll(cm, shift, axis=0), -jnp.inf))
            shift *= 2
        m0b = jnp.broadcast_to(m_in[...], (nb, tm, LANES)).reshape(rows, LANES)
        cmx = jnp.maximum(m0b, cm)
        p = -cmx
        m = bt + cmx
        r.m[...] = m.reshape(nb, tm, LANES)[:, tm - 1:tm, :]
        nchunk = rows // seg
        p3 = p.reshape(nchunk, seg, LANES)
        p_before = jnp.where(rin == 0, -m0b, pltpu.roll(p, 1, axis=0)).reshape(nchunk, seg, LANES)[:, 0:1, :]
        r.p_s[...] = p
        r.g_s[...] = jnp.exp(p3 - p_before).reshape(rows, LANES)
        r.we_s[...] = jnp.exp(p3[:, seg - 1:seg, :] + rr.reshape(nchunk, seg, LANES)).reshape(rows, LANES)
        r.em_s[...] = jnp.exp(-m)
        for bi in range(nblk):
            r.rt_s[:, bi * BLK:(bi + 1) * BLK] = rr[bi * BLK:(bi + 1) * BLK, :].T

    def k_proj():
        r.k_s[...] = (proj(OFF_K, M_WIDTH) * (M_DK ** -0.5)).astype(BF16)

    def v_proj():
        r.v_s[...] = proj(OFF_V, M_WIDTH).astype(BF16)

    def o_proj():
        r.og_s[...] = jax.nn.sigmoid(proj(OFF_O, M_WIDTH))

    def block_diag(a, rowseg):
        if nseg == 1:
            return a
        zero = jnp.zeros_like(a)
        return jnp.concatenate([jnp.where(rowseg == j, a, zero) for j in range(nseg)], axis=1)

    def block_consts(bi):
        ri = lax.broadcasted_iota(jnp.int32, (BLK, BLK), 0)
        ci = lax.broadcasted_iota(jnp.int32, (BLK, BLK), 1)
        sh = seg.bit_length() - 1
        rowseg = ri >> sh
        mask = ci <= ri
        if nseg > 1:
            mask = mask & (rowseg == (ci >> sh))
        r0 = bi * BLK
        rsl = slice(r0, r0 + BLK)
        seq0 = r0 // tm
        return rowseg, mask, rsl, slice(seq0, seq0 + nseg)

    def block_scores(bi):
        rowseg, mask, rsl, _ = block_consts(bi)
        hsl = [slice(h * M_DK, (h + 1) * M_DK) for h in range(M_HEADS)]
        q = [r.q_s[rsl, hs] for hs in hsl]
        k = [r.k_s[rsl, hs] for hs in hsl]
        vv = [r.v_s[rsl, hs] for hs in hsl]
        s = [lax.dot_general(q[h], k[h], (((1,), (1,)), ((), ())), preferred_element_type=F32)
             for h in range(M_HEADS)]
        sw = [s[h] * jnp.exp(jnp.where(mask, r.p_s[rsl, h:h + 1] + r.rt_s[h:h + 1, rsl], -jnp.inf))
              for h in range(M_HEADS)]
        kw = [k[h].astype(F32) * r.we_s[rsl, h:h + 1] for h in range(M_HEADS)]
        blk = types.SimpleNamespace(q=q)
        blk.rowsum = [jnp.sum(sw[h], axis=-1, keepdims=True) for h in range(M_HEADS)]
        blk.intra = [jnp.dot(sw[h].astype(BF16), vv[h], preferred_element_type=F32) for h in range(M_HEADS)]
        blk.upd = [lax.dot_general(block_diag(kw[h].astype(BF16), rowseg), vv[h], (((0,), (0,)), ((), ())),
                                   preferred_element_type=F32) for h in range(M_HEADS)]
        blk.ksum = [jnp.sum(kw[h].reshape(nseg, seg, M_DK), axis=1, keepdims=True) for h in range(M_HEADS)]
        v.blk[bi] = blk

    def block_state(bi):
        rowseg, _, rsl, ssl = block_consts(bi)
        blk = v.blk.pop(bi)
        g_blk = r.g_s[rsl, :].reshape(nseg, seg, LANES)
        heads = range(M_HEADS)
        c0 = [c_in[ssl, h] for h in heads]
        n0 = [n_in[ssl, h:h + 1, :] for h in heads]
        inter = [jnp.dot(block_diag(blk.q[h], rowseg), c0[h].astype(BF16).reshape(nseg * M_DK, M_DK),
                         preferred_element_type=F32) for h in heads]
        qn = [jnp.sum(blk.q[h].astype(F32) * jnp.broadcast_to(n0[h], (nseg, seg, M_DK)).reshape(BLK, M_DK),
                      axis=-1, keepdims=True) for h in heads]
        gc = [r.g_s[rsl, h:h + 1] for h in heads]
        num = [gc[h] * inter[h] + blk.intra[h] for h in heads]
        den = [gc[h] * qn[h] + blk.rowsum[h] for h in heads]
        hh = [num[h] * (1.0 / jnp.maximum(jnp.abs(den[h]), r.em_s[rsl, h:h + 1])) for h in heads]
        for h in heads:
            r.hn_s[rsl, h * M_DK:(h + 1) * M_DK] = _rms_rows(hh[h])
        for h in heads:
            decay = g_blk[:, seg - 1:seg, h:h + 1]
            r.c[ssl, h] = decay * c0[h] + blk.upd[h].reshape(nseg, M_DK, M_DK)
            r.n[ssl, h:h + 1, :] = decay * n0[h] + blk.ksum[h]

    def conv_inputs():
        r.ubuf_s[:, 6:8, :] = cv_in[...]
        u = proj(OFF_C, S_WIDTH) * proj(OFF_X, S_WIDTH)
        r.ubuf_s[:, 8:8 + tm, :] = u.reshape(nb, tm, S_WIDTH)
        r.cv[...] = r.ubuf_s[:, tm + 6:tm + 8, :]

    def conv_outputs():
        bg = proj(OFF_B, S_WIDTH)
        cw = r.cw[...]
        yc = (cw[0:1] * r.ubuf_s[:, 6:6 + tm, :] + cw[1:2] * r.ubuf_s[:, 7:7 + tm, :]
              + cw[2:3] * r.ubuf_s[:, 8:8 + tm, :])
        ysv = (bg.reshape(nb, tm, S_WIDTH) * yc).reshape(rows, S_WIDTH)
        for grp in range(S_GROUPS):
            lo = grp * LANES
            blk = _rms_rows(ysv[:, lo:lo + LANES])
            r.mix_s[:, M_WIDTH + lo:M_WIDTH + lo + LANES] = (
                blk * r.gmix[:, M_WIDTH + lo:M_WIDTH + lo + LANES]).astype(BF16)

    def conv_half_out():
        v.y_conv = jnp.dot(r.mix_s[:, M_WIDTH:], r.wout[M_WIDTH:, :], preferred_element_type=F32)

    def finish():
        r.mix_s[:, :M_WIDTH] = (r.hn_s[...] * r.og_s[...] * r.gmix[:, :M_WIDTH]).astype(BF16)
        y = v.y_conv + jnp.dot(r.mix_s[:, :M_WIDTH], r.wout[:M_WIDTH, :], preferred_element_type=F32)
        r.xo[...] = v.x3 + r.gt[...] * y.reshape(nb, tm, D_MODEL)

    fillers = [o_proj, conv_inputs, conv_outputs, conv_half_out]
    steps = [norm_gates, q_proj, gate_scalars, k_proj, v_proj, functools.partial(block_scores, 0)]
    for bi in range(nblk):
        if bi + 1 < nblk:
            steps.append(functools.partial(block_scores, bi + 1))
        steps.append(functools.partial(block_state, bi))
        if fillers:
            steps.append(fillers.pop(0))
    return steps + fillers + [finish]


_MIXER_REFS = (
    "x", "sh", "sc", "gt", "g1", "win", "bin", "cw", "gmix", "wout", "c0", "n0", "m0", "cv0",
    "xo", "c", "n", "m", "cv",
    "q_s", "k_s", "v_s", "og_s", "hn_s", "mix_s", "ubuf_s", "p_s", "g_s", "we_s", "em_s", "rt_s",
)


def _mixer_kernel(*refs, nb, tm, seg, nt):
    r = types.SimpleNamespace(**dict(zip(_MIXER_REFS, refs, strict=True)))
    first_tile = None if nt == 1 else pl.program_id(1) == 0
    for step in _mixer_steps(r, nb=nb, tm=tm, seg=seg, first_tile=first_tile):
        step()


def _mixer(x, mod, mod_row0, layer, w, state, *, nb, tm, name):
    bsz, t_len, _ = x.shape
    rows = nb * tm
    seg = min(BLK, tm)
    assert rows % BLK == 0 and bsz % nb == 0 and t_len % tm == 0 and mod_row0 % nb == 0
    assert tm & (tm - 1) == 0 and (nb == 1 or tm == t_len)
    c0, n0, m0, cv0, st_layer = state
    grid = (bsz // nb, t_len // tm)
    mrow = mod_row0 // nb

    def mod_spec(j):
        return pl.BlockSpec((None, None, nb, 1, D_MODEL), lambda b, t: (layer, j, mrow + b, 0, 0))

    def wspec(shape):
        nd = len(shape)
        return pl.BlockSpec((None,) + shape, lambda b, t: (layer,) + (0,) * nd)

    def sspec(shape):
        nd = len(shape)
        return pl.BlockSpec((None, nb) + shape, lambda b, t: (st_layer, b) + (0,) * nd)

    def ospec(shape):
        nd = len(shape)
        return pl.BlockSpec((nb,) + shape, lambda b, t: (b,) + (0,) * nd)

    state_shapes = ((M_HEADS, M_DK, M_DK), (M_HEADS, M_DK), (1, LANES), (CONV_W - 1, S_WIDTH))
    in_specs = [
        pl.BlockSpec((nb, tm, D_MODEL), lambda b, t: (b, t, 0)),
        mod_spec(0), mod_spec(1), mod_spec(2),
        wspec((1, D_MODEL)),
        wspec((D_MODEL, N_Z)),
        wspec((1, N_Z)),
        wspec((CONV_W, S_WIDTH)),
        wspec((1, D_MODEL)),
        wspec((D_MODEL, D_MODEL)),
    ] + [sspec(s) for s in state_shapes]
    out_shape = (jax.ShapeDtypeStruct(x.shape, F32),) + tuple(
        jax.ShapeDtypeStruct((bsz,) + s, F32) for s in state_shapes)
    out_specs = (pl.BlockSpec((nb, tm, D_MODEL), lambda b, t: (b, t, 0)),) + tuple(ospec(s) for s in state_shapes)
    scratch = [
        pltpu.VMEM((rows, M_WIDTH), BF16),
        pltpu.VMEM((rows, M_WIDTH), BF16),
        pltpu.VMEM((rows, M_WIDTH), BF16),
        pltpu.VMEM((rows, M_WIDTH), F32),
        pltpu.VMEM((rows, M_WIDTH), F32),
        pltpu.VMEM((rows, D_MODEL), BF16),
        pltpu.VMEM((nb, tm + 8, S_WIDTH), F32),
        pltpu.VMEM((rows, LANES), F32),
        pltpu.VMEM((rows, LANES), F32),
        pltpu.VMEM((rows, LANES), F32),
        pltpu.VMEM((rows, LANES), F32),
        pltpu.VMEM((LANES, rows), F32),
    ]
    kern = functools.partial(_mixer_kernel, nb=nb, tm=tm, seg=seg, nt=t_len // tm)
    return pl.pallas_call(
        kern, out_shape=out_shape, grid=grid, in_specs=in_specs, out_specs=out_specs,
        scratch_shapes=scratch,
        compiler_params=pltpu.CompilerParams(dimension_semantics=("arbitrary", "arbitrary"),
                                             vmem_limit_bytes=VMEM_LIMIT),
        name=name,
    )(x, mod, mod, mod, w["g1"], w["w_in"], w["b_in"], w["conv_w"], w["g_mix"], w["w_out"],
      c0, n0, m0, cv0)


def _mlp_kernel(x_ref, sh_ref, sc_ref, gt_ref, g2_ref, wup_ref, wdn_ref, gf_ref, xo_ref, *, nb, tm, final):
    rows = nb * tm
    x3 = x_ref[...]
    h3 = _rms_rows(x3) * (g2_ref[...] * (1.0 + sc_ref[...])) + sh_ref[...]
    hb = h3.reshape(rows, D_MODEL).astype(BF16)
    acc = jnp.zeros((rows, D_MODEL), F32)
    for c in range(D_FF // FF_CHUNK):
        lo = c * FF_CHUNK
        up = jnp.dot(hb, wup_ref[:, lo:lo + FF_CHUNK], preferred_element_type=F32)
        act = jnp.square(jnp.maximum(up, 0.0)).astype(BF16)
        acc = acc + jnp.dot(act, wdn_ref[lo:lo + FF_CHUNK, :], preferred_element_type=F32)
    xn = x3 + gt_ref[...] * acc.reshape(nb, tm, D_MODEL)
    if final:
        xn = _rms_rows(xn) * gf_ref[...]
    xo_ref[...] = xn


def _mlp(x, mod, mod_row0, layer, w, g_final, *, nb, tm, final, name):
    bsz, t_len, _ = x.shape
    assert bsz % nb == 0 and t_len % tm == 0 and mod_row0 % nb == 0
    grid = (bsz // nb, t_len // tm)
    mrow = mod_row0 // nb

    def mod_spec(j):
        return pl.BlockSpec((None, None, nb, 1, D_MODEL), lambda b, t: (layer, j, mrow + b, 0, 0))

    in_specs = [
        pl.BlockSpec((nb, tm, D_MODEL), lambda b, t: (b, t, 0)),
        mod_spec(3), mod_spec(4), mod_spec(5),
        pl.BlockSpec((None, 1, D_MODEL), lambda b, t: (layer, 0, 0)),
        pl.BlockSpec((None, D_MODEL, D_FF), lambda b, t: (layer, 0, 0)),
        pl.BlockSpec((None, D_FF, D_MODEL), lambda b, t: (layer, 0, 0)),
        pl.BlockSpec((1, D_MODEL), lambda b, t: (0, 0)),
    ]
    kern = functools.partial(_mlp_kernel, nb=nb, tm=tm, final=final)
    return pl.pallas_call(
        kern, out_shape=jax.ShapeDtypeStruct(x.shape, F32), grid=grid, in_specs=in_specs,
        out_specs=pl.BlockSpec((nb, tm, D_MODEL), lambda b, t: (b, t, 0)),
        compiler_params=pltpu.CompilerParams(dimension_semantics=("arbitrary", "arbitrary"),
                                             vmem_limit_bytes=VMEM_LIMIT),
        name=name,
    )(x, mod, mod, mod, w["g2"], w["w_up"], w["w_down"], g_final)


def kernel(x_prompt, x_sample, c_prompt, c_sample, state_C, state_n, state_m, state_conv, w_ada, b_ada,
           g_norm1, w_in, b_in, conv_w, g_mix_out, w_out, g_norm2, w_up, w_down, g_final):
    bp, sp, _ = x_prompt.shape
    bs, ss, _ = x_sample.shape

    def relayout_bias(a):
        gi = a[..., REF_OFF_I:REF_OFF_F]
        gf = a[..., REF_OFF_F:REF_OFF_B]
        pad = [(0, 0)] * (a.ndim - 1) + [(0, LANES - M_HEADS)]
        return jnp.concatenate([a[..., :REF_OFF_I], a[..., REF_OFF_B:], jnp.pad(gi, pad), jnp.pad(gf, pad)], axis=-1)

    w = {
        "g1": g_norm1.reshape(DEPTH, 1, D_MODEL),
        "w_in": _relayout_w_in(w_in),
        "b_in": relayout_bias(b_in).reshape(DEPTH, 1, N_Z),
        "conv_w": conv_w,
        "g_mix": g_mix_out.reshape(DEPTH, 1, D_MODEL),
        "w_out": w_out.astype(BF16),
        "g2": g_norm2.reshape(DEPTH, 1, D_MODEL),
        "w_up": w_up.astype(BF16),
        "w_down": w_down.astype(BF16),
    }
    gfin = g_final.reshape(1, D_MODEL)

    mod = _modulation(jnp.concatenate([c_sample, c_prompt], axis=0), w_ada, b_ada)

    zero_state = (jnp.zeros((1, bp, M_HEADS, M_DK, M_DK), F32), jnp.zeros((1, bp, M_HEADS, M_DK), F32),
                  jnp.zeros((1, bp, 1, LANES), F32), jnp.zeros((1, bp, CONV_W - 1, S_WIDTH), F32))
    m_pad = jnp.pad(state_m, ((0, 0), (0, 0), (0, LANES - M_HEADS))).reshape(DEPTH, bs, 1, LANES)

    tm_p = 512
    nb_s = 16
    xp, xs = x_prompt, x_sample
    outs = [[] for _ in range(8)]
    for l in range(DEPTH):
        last = l == DEPTH - 1
        xp, c1, n1, m1, cv1 = _mixer(xp, mod, bs, l, w, zero_state + (0,), nb=1, tm=tm_p, name=f"mixer_prompt_{l}")
        xs, c2, n2, m2, cv2 = _mixer(xs, mod, 0, l, w, (state_C, state_n, m_pad, state_conv, l),
                                     nb=nb_s, tm=ss, name=f"mixer_sample_{l}")
        for lst, val in zip(outs, (c1, n1, m1[:, 0, :M_HEADS], cv1, c2, n2, m2[:, 0, :M_HEADS], cv2)):
            lst.append(val)
        xp = _mlp(xp, mod, bs, l, w, gfin, nb=1, tm=tm_p, final=last, name=f"mlp_prompt_{l}")
        xs = _mlp(xs, mod, 0, l, w, gfin, nb=bs, tm=ss, final=last, name=f"mlp_sample_{l}")
    return (xp, xs) + tuple(jnp.stack(o) for o in outs)
```

```python
import functools
import types

import jax
import jax.numpy as jnp
from jax import lax
from jax.experimental import pallas as pl
from jax.experimental.pallas import tpu as pltpu

F32 = jnp.float32
BF16 = jnp.bfloat16

D_MODEL = 1024
DEPTH = 4
M_HEADS = 4
M_DK = 128
M_WIDTH = M_HEADS * M_DK
S_WIDTH = 512
S_GROUPS = 4
CONV_W = 3
D_FF = 4 * D_MODEL
N_MOD = 6
EPS = 1e-6

REF_OFF_I = 4 * M_WIDTH
REF_OFF_F = REF_OFF_I + M_HEADS
REF_OFF_B = REF_OFF_F + M_HEADS
REF_N_IN = REF_OFF_B + 3 * S_WIDTH

LANES = 128
BLK = 128
OFF_Q, OFF_K, OFF_V, OFF_O = 0, 512, 1024, 1536
OFF_B, OFF_C, OFF_X = 2048, 2560, 3072
OFF_GI, OFF_GF = 3584, 3712
N_Z = 3840
FF_CHUNK = 1024
RELAYOUT_COLS = 256
VMEM_LIMIT = 56 * 1024 * 1024


def _rms_rows(x):
    return x * lax.rsqrt(jnp.mean(x * x, axis=-1, keepdims=True) + EPS)


def _log_sigmoid(x):
    return jnp.minimum(x, 0.0) - jnp.log1p(jnp.exp(-jnp.abs(x)))


def _mod_kernel(c_ref, w_ref, b_ref, o_ref):
    c = c_ref[...]
    a = (c * jax.nn.sigmoid(c)).astype(BF16)
    o_ref[...] = jnp.dot(a, w_ref[...].astype(BF16), preferred_element_type=F32) + b_ref[...]


def _modulation(c_all, w_ada, b_ada):
    nrow = c_all.shape[0]
    b4 = b_ada.reshape(DEPTH, N_MOD, 1, D_MODEL)
    out = pl.pallas_call(
        _mod_kernel,
        out_shape=jax.ShapeDtypeStruct((DEPTH, N_MOD, nrow, D_MODEL), F32),
        grid=(DEPTH, N_MOD),
        in_specs=[
            pl.BlockSpec((nrow, D_MODEL), lambda l, j: (0, 0)),
            pl.BlockSpec((None, D_MODEL, D_MODEL), lambda l, j: (l, 0, j)),
            pl.BlockSpec((None, None, 1, D_MODEL), lambda l, j: (l, j, 0, 0)),
        ],
        out_specs=pl.BlockSpec((None, None, nrow, D_MODEL), lambda l, j: (l, j, 0, 0)),
        compiler_params=pltpu.CompilerParams(dimension_semantics=("arbitrary", "arbitrary")),
        name="adaln_modulation",
    )(c_all, w_ada, b4)
    return out.reshape(DEPTH, N_MOD, nrow, 1, D_MODEL)


def _relayout_kernel(wt_ref, o_ref):
    def put(col0, row0, ncol):
        for c in range(0, ncol, RELAYOUT_COLS):
            o_ref[:, col0 + c:col0 + c + RELAYOUT_COLS] = (
                wt_ref[row0 + c:row0 + c + RELAYOUT_COLS, :].T.astype(BF16))

    put(0, 0, REF_OFF_I)
    put(OFF_B, REF_OFF_B, REF_N_IN - REF_OFF_B)
    g = wt_ref[REF_OFF_I:REF_OFF_I + LANES, :].T
    lane = lax.broadcasted_iota(jnp.int32, g.shape, 1)
    o_ref[:, OFF_GI:OFF_GF] = jnp.where(lane < M_HEADS, g, 0.0).astype(BF16)
    o_ref[:, OFF_GF:N_Z] = jnp.where(lane < M_HEADS, pltpu.roll(g, LANES - M_HEADS, axis=1), 0.0).astype(BF16)


def _relayout_w_in(w_in):
    wt = jnp.swapaxes(w_in, 1, 2)
    return pl.pallas_call(
        _relayout_kernel,
        out_shape=jax.ShapeDtypeStruct((DEPTH, D_MODEL, N_Z), BF16),
        grid=(DEPTH,),
        in_specs=[pl.BlockSpec((None, REF_N_IN, D_MODEL), lambda l: (l, 0, 0))],
        out_specs=pl.BlockSpec((None, D_MODEL, N_Z), lambda l: (l, 0, 0)),
        compiler_params=pltpu.CompilerParams(dimension_semantics=("arbitrary",), vmem_limit_bytes=VMEM_LIMIT),
        name="relayout_w_in",
    )(wt)


def _mixer_steps(r, *, nb, tm, seg, first_tile):
    rows = nb * tm
    nblk = rows // BLK
    nseg = BLK // seg
    v = types.SimpleNamespace(blk={})

    if first_tile is None:
        c_in, n_in, m_in, cv_in = r.c0, r.n0, r.m0, r.cv0
    else:
        c_in, n_in, m_in, cv_in = r.c, r.n, r.m, r.cv

    def proj(off, width):
        return (jnp.dot(v.hb, r.win[:, off:off + width], preferred_element_type=F32)
                + r.bin[:, off:off + width])

    def norm_gates():
        if first_tile is not None:
            @pl.when(first_tile)
            def _():
                r.c[...] = r.c0[...]
                r.n[...] = r.n0[...]
                r.m[...] = r.m0[...]
                r.cv[...] = r.cv0[...]

        v.x3 = r.x[...]
        h3 = _rms_rows(v.x3) * (r.g1[...] * (1.0 + r.sc[...])) + r.sh[...]
        v.hb = h3.reshape(rows, D_MODEL).astype(BF16)
        v.gates = proj(OFF_GI, 2 * LANES)

    def q_proj():
        r.q_s[...] = proj(OFF_Q, M_WIDTH).astype(BF16)

    def gate_scalars():
        logi = v.gates[:, :LANES]
        logf = _log_sigmoid(v.gates[:, LANES:])
        rin = lax.broadcasted_iota(jnp.int32, (rows, LANES), 0) & (tm - 1)
        bt = logf
        shift = 1
        while shift < tm:
            bt = bt + jnp.where(rin >= shift, pltpu.roll(bt, shift, axis=0), 0.0)
            shift *= 2
        rr = logi - bt
        cm = rr
        shift = 1
        while shift < tm:
            cm = jnp.maximum(cm, jnp.where(rin >= shift, pltpu.roll(cm, shift, axis=0), -jnp.inf))
            shift *= 2
        m0b = jnp.broadcast_to(m_in[...], (nb, tm, LANES)).reshape(rows, LANES)
        cmx = jnp.maximum(m0b, cm)
        p = -cmx
        m = bt + cmx
        r.m[...] = m.reshape(nb, tm, LANES)[:, tm - 1:tm, :]
        nchunk = rows // seg
        p3 = p.reshape(nchunk, seg, LANES)
        p_before = jnp.where(rin == 0, -m0b, pltpu.roll(p, 1, axis=0)).reshape(nchunk, seg, LANES)[:, 0:1, :]
        r.p_s[...] = p
        r.g_s[...] = jnp.exp(p3 - p_before).reshape(rows, LANES)
        r.we_s[...] = jnp.exp(p3[:, seg - 1:seg, :] + rr.reshape(nchunk, seg, LANES)).reshape(rows, LANES)
        r.em_s[...] = jnp.exp(-m)
        for bi in range(nblk):
            r.rt_s[:, bi * BLK:(bi + 1) * BLK] = rr[bi * BLK:(bi + 1) * BLK, :].T

    def k_proj():
        r.k_s[...] = (proj(OFF_K, M_WIDTH) * (M_DK ** -0.5)).astype(BF16)

    def v_proj():
        r.v_s[...] = proj(OFF_V, M_WIDTH).astype(BF16)

    def o_proj():
        r.og_s[...] = jax.nn.sigmoid(proj(OFF_O, M_WIDTH))

    def block_diag(a, rowseg):
        if nseg == 1:
            return a
        zero = jnp.zeros_like(a)
        return jnp.concatenate([jnp.where(rowseg == j, a, zero) for j in range(nseg)], axis=1)

    def block_consts(bi):
        ri = lax.broadcasted_iota(jnp.int32, (BLK, BLK), 0)
        ci = lax.broadcasted_iota(jnp.int32, (BLK, BLK), 1)
        sh = seg.bit_length() - 1
        rowseg = ri >> sh
        mask = ci <= ri
        if nseg > 1:
            mask = mask & (rowseg == (ci >> sh))
        r0 = bi * BLK
        rsl = slice(r0, r0 + BLK)
        seq0 = r0 // tm
        return rowseg, mask, rsl, slice(seq0, seq0 + nseg)

    def block_scores(bi):
        rowseg, mask, rsl, _ = block_consts(bi)
        hsl = [slice(h * M_DK, (h + 1) * M_DK) for h in range(M_HEADS)]
        q = [r.q_s[rsl, hs] for hs in hsl]
        k = [r.k_s[rsl, hs] for hs in hsl]
        vv = [r.v_s[rsl, hs] for hs in hsl]
        s = [lax.dot_general(q[h], k[h], (((1,), (1,)), ((), ())), preferred_element_type=F32)
             for h in range(M_HEADS)]
        sw = [s[h] * jnp.exp(jnp.where(mask, r.p_s[rsl, h:h + 1] + r.rt_s[h:h + 1, rsl], -jnp.inf))
              for h in range(M_HEADS)]
        kw = [k[h].astype(F32) * r.we_s[rsl, h:h + 1] for h in range(M_HEADS)]
        blk = types.SimpleNamespace(q=q)
        blk.rowsum = [jnp.sum(sw[h], axis=-1, keepdims=True) for h in range(M_HEADS)]
        blk.intra = [jnp.dot(sw[h].astype(BF16), vv[h], preferred_element_type=F32) for h in range(M_HEADS)]
        blk.upd = [lax.dot_general(block_diag(kw[h].astype(BF16), rowseg), vv[h], (((0,), (0,)), ((), ())),
                                   preferred_element_type=F32) for h in range(M_HEADS)]
        blk.ksum = [jnp.sum(kw[h].reshape(nseg, seg, M_DK), axis=1, keepdims=True) for h in range(M_HEADS)]
        v.blk[bi] = blk

    def block_state(bi):
        rowseg, _, rsl, ssl = block_consts(bi)
        blk = v.blk.pop(bi)
        g_blk = r.g_s[rsl, :].reshape(nseg, seg, LANES)
        heads = range(M_HEADS)
        c0 = [c_in[ssl, h] for h in heads]
        n0 = [n_in[ssl, h:h + 1, :] for h in heads]
        inter = [jnp.dot(block_diag(blk.q[h], rowseg), c0[h].astype(BF16).reshape(nseg * M_DK, M_DK),
                         preferred_element_type=F32) for h in heads]
        qn = [jnp.sum(blk.q[h].astype(F32) * jnp.broadcast_to(n0[h], (nseg, seg, M_DK)).reshape(BLK, M_DK),
                      axis=-1, keepdims=True) for h in heads]
        gc = [r.g_s[rsl, h:h + 1] for h in heads]
        num = [gc[h] * inter[h] + blk.intra[h] for h in heads]
        den = [gc[h] * qn[h] + blk.rowsum[h] for h in heads]
        hh = [num[h] * (1.0 / jnp.maximum(jnp.abs(den[h]), r.em_s[rsl, h:h + 1])) for h in heads]
        for h in heads:
            r.hn_s[rsl, h * M_DK:(h + 1) * M_DK] = _rms_rows(hh[h])
        for h in heads:
            decay = g_blk[:, seg - 1:seg, h:h + 1]
            r.c[ssl, h] = decay * c0[h] + blk.upd[h].reshape(nseg, M_DK, M_DK)
            r.n[ssl, h:h + 1, :] = decay * n0[h] + blk.ksum[h]

    def conv_inputs():
        r.ubuf_s[:, 6:8, :] = cv_in[...]
        u = proj(OFF_C, S_WIDTH) * proj(OFF_X, S_WIDTH)
        r.ubuf_s[:, 8:8 + tm, :] = u.reshape(nb, tm, S_WIDTH)
        r.cv[...] = r.ubuf_s[:, tm + 6:tm + 8, :]

    def conv_outputs():
        bg = proj(OFF_B, S_WIDTH)
        cw = r.cw[...]
        yc = (cw[0:1] * r.ubuf_s[:, 6:6 + tm, :] + cw[1:2] * r.ubuf_s[:, 7:7 + tm, :]
              + cw[2:3] * r.ubuf_s[:, 8:8 + tm, :])
        ysv = (bg.reshape(nb, tm, S_WIDTH) * yc).reshape(rows, S_WIDTH)
        for grp in range(S_GROUPS):
            lo = grp * LANES
            blk = _rms_rows(ysv[:, lo:lo + LANES])
            r.mix_s[:, M_WIDTH + lo:M_WIDTH + lo + LANES] = (
                blk * r.gmix[:, M_WIDTH + lo:M_WIDTH + lo + LANES]).astype(BF16)

    def conv_half_out():
        v.y_conv = jnp.dot(r.mix_s[:, M_WIDTH:], r.wout[M_WIDTH:, :], preferred_element_type=F32)

    def finish():
        r.mix_s[:, :M_WIDTH] = (r.hn_s[...] * r.og_s[...] * r.gmix[:, :M_WIDTH]).astype(BF16)
        y = v.y_conv + jnp.dot(r.mix_s[:, :M_WIDTH], r.wout[:M_WIDTH, :], preferred_element_type=F32)
        r.xo[...] = v.x3 + r.gt[...] * y.reshape(nb, tm, D_MODEL)

    fillers = [o_proj, conv_inputs, conv_outputs, conv_half_out]
    steps = [norm_gates, q_proj, gate_scalars, k_proj, v_proj, functools.partial(block_scores, 0)]
    for bi in range(nblk):
        if bi + 1 < nblk:
            steps.append(functools.partial(block_scores, bi + 1))
        steps.append(functools.partial(block_state, bi))
        if fillers:
            steps.append(fillers.pop(0))
    return steps + fillers + [finish]


_MIXER_REFS = (
    "x", "sh", "sc", "gt", "g1", "win", "bin", "cw", "gmix", "wout", "c0", "n0", "m0", "cv0",
    "xo", "c", "n", "m", "cv",
    "q_s", "k_s", "v_s", "og_s", "hn_s", "mix_s", "ubuf_s", "p_s", "g_s", "we_s", "em_s", "rt_s",
)


def _mixer_kernel(*refs, nb, tm, seg, nt):
    r = types.SimpleNamespace(**dict(zip(_MIXER_REFS, refs, strict=True)))
    first_tile = None if nt == 1 else pl.program_id(1) == 0
    for step in _mixer_steps(r, nb=nb, tm=tm, seg=seg, first_tile=first_tile):
        step()


def _mixer(x, mod, mod_row0, layer, w, state, *, nb, tm, name):
    bsz, t_len, _ = x.shape
    rows = nb * tm
    seg = min(BLK, tm)
    assert rows % BLK == 0 and bsz % nb == 0 and t_len % tm == 0 and mod_row0 % nb == 0
    assert tm & (tm - 1) == 0 and (nb == 1 or tm == t_len)
    c0, n0, m0, cv0, st_layer = state
    grid = (bsz // nb, t_len // tm)
    mrow = mod_row0 // nb

    def mod_spec(j):
        return pl.BlockSpec((None, None, nb, 1, D_MODEL), lambda b, t: (layer, j, mrow + b, 0, 0))

    def wspec(shape, resident=False):
        nd = len(shape)
        mode = {"pipeline_mode": pl.Buffered(1)} if resident else {}
        return pl.BlockSpec((None,) + shape, lambda b, t: (layer,) + (0,) * nd, **mode)

    def sspec(shape):
        nd = len(shape)
        return pl.BlockSpec((None, nb) + shape, lambda b, t: (st_layer, b) + (0,) * nd)

    def ospec(shape):
        nd = len(shape)
        return pl.BlockSpec((nb,) + shape, lambda b, t: (b,) + (0,) * nd)

    state_shapes = ((M_HEADS, M_DK, M_DK), (M_HEADS, M_DK), (1, LANES), (CONV_W - 1, S_WIDTH))
    in_specs = [
        pl.BlockSpec((nb, tm, D_MODEL), lambda b, t: (b, t, 0)),
        mod_spec(0), mod_spec(1), mod_spec(2),
        wspec((1, D_MODEL)),
        wspec((D_MODEL, N_Z), True),
        wspec((1, N_Z)),
        wspec((CONV_W, S_WIDTH)),
        wspec((1, D_MODEL)),
        wspec((D_MODEL, D_MODEL), True),
    ] + [sspec(s) for s in state_shapes]
    out_shape = (jax.ShapeDtypeStruct(x.shape, F32),) + tuple(
        jax.ShapeDtypeStruct((bsz,) + s, F32) for s in state_shapes)
    out_specs = (pl.BlockSpec((nb, tm, D_MODEL), lambda b, t: (b, t, 0)),) + tuple(ospec(s) for s in state_shapes)
    scratch = [
        pltpu.VMEM((rows, M_WIDTH), BF16),
        pltpu.VMEM((rows, M_WIDTH), BF16),
        pltpu.VMEM((rows, M_WIDTH), BF16),
        pltpu.VMEM((rows, M_WIDTH), F32),
        pltpu.VMEM((rows, M_WIDTH), F32),
        pltpu.VMEM((rows, D_MODEL), BF16),
        pltpu.VMEM((nb, tm + 8, S_WIDTH), F32),
        pltpu.VMEM((rows, LANES), F32),
        pltpu.VMEM((rows, LANES), F32),
        pltpu.VMEM((rows, LANES), F32),
        pltpu.VMEM((rows, LANES), F32),
        pltpu.VMEM((LANES, rows), F32),
    ]
    kern = functools.partial(_mixer_kernel, nb=nb, tm=tm, seg=seg, nt=t_len // tm)
    return pl.pallas_call(
        kern, out_shape=out_shape, grid=grid, in_specs=in_specs, out_specs=out_specs,
        scratch_shapes=scratch,
        compiler_params=pltpu.CompilerParams(dimension_semantics=("arbitrary", "arbitrary"),
                                             vmem_limit_bytes=VMEM_LIMIT),
        name=name,
    )(x, mod, mod, mod, w["g1"], w["w_in"], w["b_in"], w["conv_w"], w["g_mix"], w["w_out"],
      c0, n0, m0, cv0)


def _mlp_kernel(x_ref, sh_ref, sc_ref, gt_ref, g2_ref, wup_ref, wdn_ref, gf_ref, xo_ref, *, nb, tm, final):
    rows = nb * tm
    x3 = x_ref[...]
    h3 = _rms_rows(x3) * (g2_ref[...] * (1.0 + sc_ref[...])) + sh_ref[...]
    hb = h3.reshape(rows, D_MODEL).astype(BF16)
    acc = jnp.zeros((rows, D_MODEL), F32)
    for c in range(D_FF // FF_CHUNK):
        lo = c * FF_CHUNK
        up = jnp.dot(hb, wup_ref[:, lo:lo + FF_CHUNK], preferred_element_type=F32)
        act = jnp.square(jnp.maximum(up, 0.0)).astype(BF16)
        acc = acc + jnp.dot(act, wdn_ref[lo:lo + FF_CHUNK, :], preferred_element_type=F32)
    xn = x3 + gt_ref[...] * acc.reshape(nb, tm, D_MODEL)
    if final:
        xn = _rms_rows(xn) * gf_ref[...]
    xo_ref[...] = xn


def _mlp(x, mod, mod_row0, layer, w, g_final, *, nb, tm, final, name):
    bsz, t_len, _ = x.shape
    assert bsz % nb == 0 and t_len % tm == 0 and mod_row0 % nb == 0
    grid = (bsz // nb, t_len // tm)
    mrow = mod_row0 // nb

    def mod_spec(j):
        return pl.BlockSpec((None, None, nb, 1, D_MODEL), lambda b, t: (layer, j, mrow + b, 0, 0))

    in_specs = [
        pl.BlockSpec((nb, tm, D_MODEL), lambda b, t: (b, t, 0)),
        mod_spec(3), mod_spec(4), mod_spec(5),
        pl.BlockSpec((None, 1, D_MODEL), lambda b, t: (layer, 0, 0)),
        pl.BlockSpec((None, D_MODEL, D_FF), lambda b, t: (layer, 0, 0), pipeline_mode=pl.Buffered(1)),
        pl.BlockSpec((None, D_FF, D_MODEL), lambda b, t: (layer, 0, 0), pipeline_mode=pl.Buffered(1)),
        pl.BlockSpec((1, D_MODEL), lambda b, t: (0, 0)),
    ]
    kern = functools.partial(_mlp_kernel, nb=nb, tm=tm, final=final)
    return pl.pallas_call(
        kern, out_shape=jax.ShapeDtypeStruct(x.shape, F32), grid=grid, in_specs=in_specs,
        out_specs=pl.BlockSpec((nb, tm, D_MODEL), lambda b, t: (b, t, 0)),
        compiler_params=pltpu.CompilerParams(dimension_semantics=("arbitrary", "arbitrary"),
                                             vmem_limit_bytes=VMEM_LIMIT),
        name=name,
    )(x, mod, mod, mod, w["g2"], w["w_up"], w["w_down"], g_final)


def kernel(x_prompt, x_sample, c_prompt, c_sample, state_C, state_n, state_m, state_conv, w_ada, b_ada,
           g_norm1, w_in, b_in, conv_w, g_mix_out, w_out, g_norm2, w_up, w_down, g_final):
    bp, sp, _ = x_prompt.shape
    bs, ss, _ = x_sample.shape

    def relayout_bias(a):
        gi = a[..., REF_OFF_I:REF_OFF_F]
        gf = a[..., REF_OFF_F:REF_OFF_B]
        pad = [(0, 0)] * (a.ndim - 1) + [(0, LANES - M_HEADS)]
        return jnp.concatenate([a[..., :REF_OFF_I], a[..., REF_OFF_B:], jnp.pad(gi, pad), jnp.pad(gf, pad)], axis=-1)

    w = {
        "g1": g_norm1.reshape(DEPTH, 1, D_MODEL),
        "w_in": _relayout_w_in(w_in),
        "b_in": relayout_bias(b_in).reshape(DEPTH, 1, N_Z),
        "conv_w": conv_w,
        "g_mix": g_mix_out.reshape(DEPTH, 1, D_MODEL),
        "w_out": w_out.astype(BF16),
        "g2": g_norm2.reshape(DEPTH, 1, D_MODEL),
        "w_up": w_up.astype(BF16),
        "w_down": w_down.astype(BF16),
    }
    gfin = g_final.reshape(1, D_MODEL)

    mod = _modulation(jnp.concatenate([c_sample, c_prompt], axis=0), w_ada, b_ada)

    zero_state = (jnp.zeros((1, bp, M_HEADS, M_DK, M_DK), F32), jnp.zeros((1, bp, M_HEADS, M_DK), F32),
                  jnp.zeros((1, bp, 1, LANES), F32), jnp.zeros((1, bp, CONV_W - 1, S_WIDTH), F32))
    m_pad = jnp.pad(state_m, ((0, 0), (0, 0), (0, LANES - M_HEADS))).reshape(DEPTH, bs, 1, LANES)

    tm_p = 1024
    nb_s = 16
    xp, xs = x_prompt, x_sample
    outs = [[] for _ in range(8)]
    for l in range(DEPTH):
        last = l == DEPTH - 1
        xp, c1, n1, m1, cv1 = _mixer(xp, mod, bs, l, w, zero_state + (0,), nb=1, tm=tm_p, name=f"mixer_prompt_{l}")
        xs, c2, n2, m2, cv2 = _mixer(xs, mod, 0, l, w, (state_C, state_n, m_pad, state_conv, l),
                                     nb=nb_s, tm=ss, name=f"mixer_sample_{l}")
        for lst, val in zip(outs, (c1, n1, m1[:, 0, :M_HEADS], cv1, c2, n2, m2[:, 0, :M_HEADS], cv2)):
            lst.append(val)
        xp = _mlp(xp, mod, bs, l, w, gfin, nb=1, tm=tm_p, final=last, name=f"mlp_prompt_{l}")
        xs = _mlp(xs, mod, 0, l, w, gfin, nb=bs, tm=ss, final=last, name=f"mlp_sample_{l}")
    return (xp, xs) + tuple(jnp.stack(o) for o in outs)
```

```python
import functools
import types

import jax
import jax.numpy as jnp
from jax import lax
from jax.experimental import pallas as pl
from jax.experimental.pallas import tpu as pltpu

F32 = jnp.float32
BF16 = jnp.bfloat16

D_MODEL = 1024
DEPTH = 4
M_HEADS = 4
M_DK = 128
M_WIDTH = M_HEADS * M_DK
S_WIDTH = 512
S_GROUPS = 4
CONV_W = 3
D_FF = 4 * D_MODEL
N_MOD = 6
EPS = 1e-6

REF_OFF_I = 4 * M_WIDTH
REF_OFF_F = REF_OFF_I + M_HEADS
REF_OFF_B = REF_OFF_F + M_HEADS
REF_N_IN = REF_OFF_B + 3 * S_WIDTH

LANES = 128
BLK = 128
OFF_Q, OFF_K, OFF_V, OFF_O = 0, 512, 1024, 1536
OFF_B, OFF_C, OFF_X = 2048, 2560, 3072
OFF_GI, OFF_GF = 3584, 3712
N_Z = 3840
FF_CHUNK = 1024
RELAYOUT_COLS = 256
VMEM_LIMIT = 56 * 1024 * 1024


def _rms_rows(x):
    return x * lax.rsqrt(jnp.mean(x * x, axis=-1, keepdims=True) + EPS)


def _log_sigmoid(x):
    return jnp.minimum(x, 0.0) - jnp.log1p(jnp.exp(-jnp.abs(x)))


def _mod_kernel(c_ref, w_ref, b_ref, o_ref):
    c = c_ref[...]
    a = (c * jax.nn.sigmoid(c)).astype(BF16)
    o_ref[...] = jnp.dot(a, w_ref[...].astype(BF16), preferred_element_type=F32) + b_ref[...]


def _modulation(c_all, w_ada, b_ada):
    nrow = c_all.shape[0]
    b4 = b_ada.reshape(DEPTH, N_MOD, 1, D_MODEL)
    out = pl.pallas_call(
        _mod_kernel,
        out_shape=jax.ShapeDtypeStruct((DEPTH, N_MOD, nrow, D_MODEL), F32),
        grid=(DEPTH, N_MOD),
        in_specs=[
            pl.BlockSpec((nrow, D_MODEL), lambda l, j: (0, 0)),
            pl.BlockSpec((None, D_MODEL, D_MODEL), lambda l, j: (l, 0, j)),
            pl.BlockSpec((None, None, 1, D_MODEL), lambda l, j: (l, j, 0, 0)),
        ],
        out_specs=pl.BlockSpec((None, None, nrow, D_MODEL), lambda l, j: (l, j, 0, 0)),
        compiler_params=pltpu.CompilerParams(dimension_semantics=("arbitrary", "arbitrary")),
        name="adaln_modulation",
    )(c_all, w_ada, b4)
    return out.reshape(DEPTH, N_MOD, nrow, 1, D_MODEL)


def _relayout_kernel(wt_ref, o_ref):
    def put(col0, row0, ncol):
        for c in range(0, ncol, RELAYOUT_COLS):
            o_ref[:, col0 + c:col0 + c + RELAYOUT_COLS] = (
                wt_ref[row0 + c:row0 + c + RELAYOUT_COLS, :].T.astype(BF16))

    put(0, 0, REF_OFF_I)
    put(OFF_B, REF_OFF_B, REF_N_IN - REF_OFF_B)
    g = wt_ref[REF_OFF_I:REF_OFF_I + LANES, :].T
    lane = lax.broadcasted_iota(jnp.int32, g.shape, 1)
    o_ref[:, OFF_GI:OFF_GF] = jnp.where(lane < M_HEADS, g, 0.0).astype(BF16)
    o_ref[:, OFF_GF:N_Z] = jnp.where(lane < M_HEADS, pltpu.roll(g, LANES - M_HEADS, axis=1), 0.0).astype(BF16)


def _relayout_w_in(w_in):
    wt = jnp.swapaxes(w_in, 1, 2)
    return pl.pallas_call(
        _relayout_kernel,
        out_shape=jax.ShapeDtypeStruct((DEPTH, D_MODEL, N_Z), BF16),
        grid=(DEPTH,),
        in_specs=[pl.BlockSpec((None, REF_N_IN, D_MODEL), lambda l: (l, 0, 0))],
        out_specs=pl.BlockSpec((None, D_MODEL, N_Z), lambda l: (l, 0, 0)),
        compiler_params=pltpu.CompilerParams(dimension_semantics=("arbitrary",), vmem_limit_bytes=VMEM_LIMIT),
        name="relayout_w_in",
    )(wt)


def _mixer_part(r, *, nb, tm, seg, first_tile, lo, th):
    assert th == tm or nb == 1
    rows = nb * th
    fsl = slice(lo, lo + rows)
    tsl = slice(lo, lo + th)
    nblk = rows // BLK
    blk0 = lo // BLK
    nseg = BLK // seg
    v = types.SimpleNamespace(blk={})

    if first_tile is None:
        assert lo == 0
        c_in, n_in, m_in, cv_in = r.c0, r.n0, r.m0, r.cv0
    else:
        c_in, n_in, m_in, cv_in = r.c, r.n, r.m, r.cv

    def proj(off, width):
        return (jnp.dot(v.hb, r.win[:, off:off + width], preferred_element_type=F32)
                + r.bin[:, off:off + width])

    def norm_gates():
        if first_tile is not None and lo == 0:
            @pl.when(first_tile)
            def _():
                r.c[...] = r.c0[...]
                r.n[...] = r.n0[...]
                r.m[...] = r.m0[...]
                r.cv[...] = r.cv0[...]

        v.x3 = r.x[:, tsl, :]
        h3 = _rms_rows(v.x3) * (r.g1[...] * (1.0 + r.sc[...])) + r.sh[...]
        v.hb = h3.reshape(rows, D_MODEL).astype(BF16)
        v.gates = proj(OFF_GI, 2 * LANES)

    def q_proj():
        r.q_s[fsl, :] = proj(OFF_Q, M_WIDTH).astype(BF16)

    def gate_scalars():
        logi = v.gates[:, :LANES]
        logf = _log_sigmoid(v.gates[:, LANES:])
        rin = lax.broadcasted_iota(jnp.int32, (rows, LANES), 0) & (th - 1)
        bt = logf
        shift = 1
        while shift < th:
            bt = bt + jnp.where(rin >= shift, pltpu.roll(bt, shift, axis=0), 0.0)
            shift *= 2
        rr = logi - bt
        cm = rr
        shift = 1
        while shift < th:
            cm = jnp.maximum(cm, jnp.where(rin >= shift, pltpu.roll(cm, shift, axis=0), -jnp.inf))
            shift *= 2
        m0b = jnp.broadcast_to(m_in[...], (nb, th, LANES)).reshape(rows, LANES)
        cmx = jnp.maximum(m0b, cm)
        p = -cmx
        m = bt + cmx
        r.m[...] = m.reshape(nb, th, LANES)[:, th - 1:th, :]
        nchunk = rows // seg
        p3 = p.reshape(nchunk, seg, LANES)
        p_before = jnp.where(rin == 0, -m0b, pltpu.roll(p, 1, axis=0)).reshape(nchunk, seg, LANES)[:, 0:1, :]
        r.p_s[fsl, :] = p
        r.g_s[fsl, :] = jnp.exp(p3 - p_before).reshape(rows, LANES)
        r.we_s[fsl, :] = jnp.exp(p3[:, seg - 1:seg, :] + rr.reshape(nchunk, seg, LANES)).reshape(rows, LANES)
        r.em_s[fsl, :] = jnp.exp(-m)
        for bi in range(nblk):
            r.rt_s[:, lo + bi * BLK:lo + (bi + 1) * BLK] = rr[bi * BLK:(bi + 1) * BLK, :].T

    def k_proj():
        r.k_s[fsl, :] = (proj(OFF_K, M_WIDTH) * (M_DK ** -0.5)).astype(BF16)

    def v_proj():
        r.v_s[fsl, :] = proj(OFF_V, M_WIDTH).astype(BF16)

    def o_proj():
        r.og_s[fsl, :] = jax.nn.sigmoid(proj(OFF_O, M_WIDTH))

    def block_diag(a, rowseg):
        if nseg == 1:
            return a
        zero = jnp.zeros_like(a)
        return jnp.concatenate([jnp.where(rowseg == j, a, zero) for j in range(nseg)], axis=1)

    def block_consts(bi):
        ri = lax.broadcasted_iota(jnp.int32, (BLK, BLK), 0)
        ci = lax.broadcasted_iota(jnp.int32, (BLK, BLK), 1)
        sh = seg.bit_length() - 1
        rowseg = ri >> sh
        mask = ci <= ri
        if nseg > 1:
            mask = mask & (rowseg == (ci >> sh))
        r0 = bi * BLK
        rsl = slice(r0, r0 + BLK)
        seq0 = r0 // tm
        return rowseg, mask, rsl, slice(seq0, seq0 + nseg)

    def block_scores(bi):
        rowseg, mask, rsl, _ = block_consts(bi)
        hsl = [slice(h * M_DK, (h + 1) * M_DK) for h in range(M_HEADS)]
        q = [r.q_s[rsl, hs] for hs in hsl]
        k = [r.k_s[rsl, hs] for hs in hsl]
        vv = [r.v_s[rsl, hs] for hs in hsl]
        s = [lax.dot_general(q[h], k[h], (((1,), (1,)), ((), ())), preferred_element_type=F32)
             for h in range(M_HEADS)]
        sw = [s[h] * jnp.exp(jnp.where(mask, r.p_s[rsl, h:h + 1] + r.rt_s[h:h + 1, rsl], -jnp.inf))
              for h in range(M_HEADS)]
        kw = [k[h].astype(F32) * r.we_s[rsl, h:h + 1] for h in range(M_HEADS)]
        blk = types.SimpleNamespace(q=q)
        blk.rowsum = [jnp.sum(sw[h], axis=-1, keepdims=True) for h in range(M_HEADS)]
        blk.intra = [jnp.dot(sw[h].astype(BF16), vv[h], preferred_element_type=F32) for h in range(M_HEADS)]
        blk.upd = [lax.dot_general(block_diag(kw[h].astype(BF16), rowseg), vv[h], (((0,), (0,)), ((), ())),
                                   preferred_element_type=F32) for h in range(M_HEADS)]
        blk.ksum = [jnp.sum(kw[h].reshape(nseg, seg, M_DK), axis=1, keepdims=True) for h in range(M_HEADS)]
        v.blk[bi] = blk

    def block_state(bi):
        rowseg, _, rsl, ssl = block_consts(bi)
        blk = v.blk.pop(bi)
        g_blk = r.g_s[rsl, :].reshape(nseg, seg, LANES)
        heads = range(M_HEADS)
        c0 = [c_in[ssl, h] for h in heads]
        n0 = [n_in[ssl, h:h + 1, :] for h in heads]
        inter = [jnp.dot(block_diag(blk.q[h], rowseg), c0[h].astype(BF16).reshape(nseg * M_DK, M_DK),
                         preferred_element_type=F32) for h in heads]
        qn = [jnp.sum(blk.q[h].astype(F32) * jnp.broadcast_to(n0[h], (nseg, seg, M_DK)).reshape(BLK, M_DK),
                      axis=-1, keepdims=True) for h in heads]
        gc = [r.g_s[rsl, h:h + 1] for h in heads]
        num = [gc[h] * inter[h] + blk.intra[h] for h in heads]
        den = [gc[h] * qn[h] + blk.rowsum[h] for h in heads]
        hh = [num[h] * (1.0 / jnp.maximum(jnp.abs(den[h]), r.em_s[rsl, h:h + 1])) for h in heads]
        for h in heads:
            r.hn_s[rsl, h * M_DK:(h + 1) * M_DK] = _rms_rows(hh[h])
        for h in heads:
            decay = g_blk[:, seg - 1:seg, h:h + 1]
            r.c[ssl, h] = decay * c0[h] + blk.upd[h].reshape(nseg, M_DK, M_DK)
            r.n[ssl, h:h + 1, :] = decay * n0[h] + blk.ksum[h]

    def conv_inputs():
        if lo == 0:
            r.ubuf_s[:, 6:8, :] = cv_in[...]
        u = proj(OFF_C, S_WIDTH) * proj(OFF_X, S_WIDTH)
        r.ubuf_s[:, 8 + lo:8 + lo + th, :] = u.reshape(nb, th, S_WIDTH)
        r.cv[...] = r.ubuf_s[:, lo + th + 6:lo + th + 8, :]

    def conv_outputs():
        bg = proj(OFF_B, S_WIDTH)
        cw = r.cw[...]
        yc = (cw[0:1] * r.ubuf_s[:, 6 + lo:6 + lo + th, :] + cw[1:2] * r.ubuf_s[:, 7 + lo:7 + lo + th, :]
              + cw[2:3] * r.ubuf_s[:, 8 + lo:8 + lo + th, :])
        ysv = (bg.reshape(nb, th, S_WIDTH) * yc).reshape(rows, S_WIDTH)
        for grp in range(S_GROUPS):
            cs = slice(M_WIDTH + grp * LANES, M_WIDTH + (grp + 1) * LANES)
            r.mix_s[fsl, cs] = (_rms_rows(ysv[:, grp * LANES:(grp + 1) * LANES]) * r.gmix[:, cs]).astype(BF16)

    def conv_half_out():
        v.y_conv = jnp.dot(r.mix_s[fsl, M_WIDTH:], r.wout[M_WIDTH:, :], preferred_element_type=F32)

    def finish():
        r.mix_s[fsl, :M_WIDTH] = (r.hn_s[fsl, :] * r.og_s[fsl, :] * r.gmix[:, :M_WIDTH]).astype(BF16)
        y = v.y_conv + jnp.dot(r.mix_s[fsl, :M_WIDTH], r.wout[:M_WIDTH, :], preferred_element_type=F32)
        r.xo[:, tsl, :] = v.x3 + r.gt[...] * y.reshape(nb, th, D_MODEL)

    blocks = [functools.partial(block_scores, blk0)]
    for bi in range(blk0, blk0 + nblk):
        if bi + 1 < blk0 + nblk:
            blocks.append(functools.partial(block_scores, bi + 1))
        blocks.append(functools.partial(block_state, bi))
    return types.SimpleNamespace(
        pre=[norm_gates, q_proj, gate_scalars, k_proj, v_proj], blocks=blocks,
        fillers=[o_proj, conv_inputs, conv_outputs, conv_half_out], finish=finish)


def _spread(chain, others):
    out, done = [], 0
    for i, step in enumerate(chain):
        out.append(step)
        want = ((i + 1) * len(others)) // len(chain)
        out.extend(others[done:want])
        done = want
    return out


def _mixer_steps(r, *, nb, tm, seg, first_tile, parts):
    th = tm // parts
    part = [_mixer_part(r, nb=nb, tm=tm, seg=seg, first_tile=first_tile, lo=i * th, th=th) for i in range(parts)]
    if parts == 1:
        a = part[0]
        return a.pre + _spread(a.blocks, a.fillers) + [a.finish]
    a, b = part
    return (a.pre
            + _spread(a.blocks, b.pre + a.fillers[:2])
            + _spread(b.blocks, a.fillers[2:] + [a.finish] + b.fillers[:2])
            + b.fillers[2:] + [b.finish])


_MIXER_REFS = (
    "x", "sh", "sc", "gt", "g1", "win", "bin", "cw", "gmix", "wout", "c0", "n0", "m0", "cv0",
    "xo", "c", "n", "m", "cv",
    "q_s", "k_s", "v_s", "og_s", "hn_s", "mix_s", "ubuf_s", "p_s", "g_s", "we_s", "em_s", "rt_s",
)


def _mixer_kernel(*refs, nb, tm, seg, nt, parts):
    r = types.SimpleNamespace(**dict(zip(_MIXER_REFS, refs, strict=True)))
    first_tile = None if nt == 1 else pl.program_id(1) == 0
    for step in _mixer_steps(r, nb=nb, tm=tm, seg=seg, first_tile=first_tile, parts=parts):
        step()


def _mixer(x, mod, mod_row0, layer, w, state, *, nb, tm, parts, name):
    bsz, t_len, _ = x.shape
    rows = nb * tm
    seg = min(BLK, tm // parts)
    assert (rows // parts) % BLK == 0 and bsz % nb == 0 and t_len % tm == 0 and mod_row0 % nb == 0
    assert tm & (tm - 1) == 0 and (nb == 1 or tm == t_len)
    c0, n0, m0, cv0, st_layer = state
    grid = (bsz // nb, t_len // tm)
    mrow = mod_row0 // nb

    def mod_spec(j):
        return pl.BlockSpec((None, None, nb, 1, D_MODEL), lambda b, t: (layer, j, mrow + b, 0, 0))

    def wspec(shape, resident=False):
        nd = len(shape)
        mode = {"pipeline_mode": pl.Buffered(1)} if resident else {}
        return pl.BlockSpec((None,) + shape, lambda b, t: (layer,) + (0,) * nd, **mode)

    def sspec(shape):
        nd = len(shape)
        return pl.BlockSpec((None, nb) + shape, lambda b, t: (st_layer, b) + (0,) * nd)

    def ospec(shape):
        nd = len(shape)
        return pl.BlockSpec((nb,) + shape, lambda b, t: (b,) + (0,) * nd)

    state_shapes = ((M_HEADS, M_DK, M_DK), (M_HEADS, M_DK), (1, LANES), (CONV_W - 1, S_WIDTH))
    in_specs = [
        pl.BlockSpec((nb, tm, D_MODEL), lambda b, t: (b, t, 0)),
        mod_spec(0), mod_spec(1), mod_spec(2),
        wspec((1, D_MODEL)),
        wspec((D_MODEL, N_Z), True),
        wspec((1, N_Z)),
        wspec((CONV_W, S_WIDTH)),
        wspec((1, D_MODEL)),
        wspec((D_MODEL, D_MODEL), True),
    ] + [sspec(s) for s in state_shapes]
    out_shape = (jax.ShapeDtypeStruct(x.shape, F32),) + tuple(
        jax.ShapeDtypeStruct((bsz,) + s, F32) for s in state_shapes)
    out_specs = (pl.BlockSpec((nb, tm, D_MODEL), lambda b, t: (b, t, 0)),) + tuple(ospec(s) for s in state_shapes)
    scratch = [
        pltpu.VMEM((rows, M_WIDTH), BF16),
        pltpu.VMEM((rows, M_WIDTH), BF16),
        pltpu.VMEM((rows, M_WIDTH), BF16),
        pltpu.VMEM((rows, M_WIDTH), F32),
        pltpu.VMEM((rows, M_WIDTH), F32),
        pltpu.VMEM((rows, D_MODEL), BF16),
        pltpu.VMEM((nb, tm + 8, S_WIDTH), F32),
        pltpu.VMEM((rows, LANES), F32),
        pltpu.VMEM((rows, LANES), F32),
        pltpu.VMEM((rows, LANES), F32),
        pltpu.VMEM((rows, LANES), F32),
        pltpu.VMEM((LANES, rows), F32),
    ]
    kern = functools.partial(_mixer_kernel, nb=nb, tm=tm, seg=seg, nt=t_len // tm, parts=parts)
    return pl.pallas_call(
        kern, out_shape=out_shape, grid=grid, in_specs=in_specs, out_specs=out_specs,
        scratch_shapes=scratch,
        compiler_params=pltpu.CompilerParams(dimension_semantics=("arbitrary", "arbitrary"),
                                             vmem_limit_bytes=VMEM_LIMIT),
        name=name,
    )(x, mod, mod, mod, w["g1"], w["w_in"], w["b_in"], w["conv_w"], w["g_mix"], w["w_out"],
      c0, n0, m0, cv0)


def _mlp_kernel(x_ref, sh_ref, sc_ref, gt_ref, g2_ref, wup_ref, wdn_ref, gf_ref, xo_ref, *, nb, tm, final):
    rows = nb * tm
    x3 = x_ref[...]
    h3 = _rms_rows(x3) * (g2_ref[...] * (1.0 + sc_ref[...])) + sh_ref[...]
    hb = h3.reshape(rows, D_MODEL).astype(BF16)
    acc = jnp.zeros((rows, D_MODEL), F32)
    for c in range(D_FF // FF_CHUNK):
        lo = c * FF_CHUNK
        up = jnp.dot(hb, wup_ref[:, lo:lo + FF_CHUNK], preferred_element_type=F32)
        act = jnp.square(jnp.maximum(up, 0.0)).astype(BF16)
        acc = acc + jnp.dot(act, wdn_ref[lo:lo + FF_CHUNK, :], preferred_element_type=F32)
    xn = x3 + gt_ref[...] * acc.reshape(nb, tm, D_MODEL)
    if final:
        xn = _rms_rows(xn) * gf_ref[...]
    xo_ref[...] = xn


def _mlp(x, mod, mod_row0, layer, w, g_final, *, nb, tm, final, name):
    bsz, t_len, _ = x.shape
    assert bsz % nb == 0 and t_len % tm == 0 and mod_row0 % nb == 0
    grid = (bsz // nb, t_len // tm)
    mrow = mod_row0 // nb

    def mod_spec(j):
        return pl.BlockSpec((None, None, nb, 1, D_MODEL), lambda b, t: (layer, j, mrow + b, 0, 0))

    in_specs = [
        pl.BlockSpec((nb, tm, D_MODEL), lambda b, t: (b, t, 0)),
        mod_spec(3), mod_spec(4), mod_spec(5),
        pl.BlockSpec((None, 1, D_MODEL), lambda b, t: (layer, 0, 0)),
        pl.BlockSpec((None, D_MODEL, D_FF), lambda b, t: (layer, 0, 0), pipeline_mode=pl.Buffered(1)),
        pl.BlockSpec((None, D_FF, D_MODEL), lambda b, t: (layer, 0, 0), pipeline_mode=pl.Buffered(1)),
        pl.BlockSpec((1, D_MODEL), lambda b, t: (0, 0)),
    ]
    kern = functools.partial(_mlp_kernel, nb=nb, tm=tm, final=final)
    return pl.pallas_call(
        kern, out_shape=jax.ShapeDtypeStruct(x.shape, F32), grid=grid, in_specs=in_specs,
        out_specs=pl.BlockSpec((nb, tm, D_MODEL), lambda b, t: (b, t, 0)),
        compiler_params=pltpu.CompilerParams(dimension_semantics=("arbitrary", "arbitrary"),
                                             vmem_limit_bytes=VMEM_LIMIT),
        name=name,
    )(x, mod, mod, mod, w["g2"], w["w_up"], w["w_down"], g_final)


def _mlp_stream_kernel(x_ref, sh_ref, sc_ref, gt_ref, g2_ref, wup_ref, wdn_ref, gf_ref, xo_ref, hb_s, acc_s,
                       *, nb, tm, final):
    rows = nb * tm
    c = pl.program_id(0)

    @pl.when(c == 0)
    def _():
        h3 = _rms_rows(x_ref[...]) * (g2_ref[...] * (1.0 + sc_ref[...])) + sh_ref[...]
        hb_s[...] = h3.reshape(rows, D_MODEL).astype(BF16)
        acc_s[...] = jnp.zeros((rows, D_MODEL), F32)

    up = jnp.dot(hb_s[...], wup_ref[...], preferred_element_type=F32)
    act = jnp.square(jnp.maximum(up, 0.0)).astype(BF16)
    acc_s[...] += jnp.dot(act, wdn_ref[...], preferred_element_type=F32)

    @pl.when(c == pl.num_programs(0) - 1)
    def _():
        xn = x_ref[...] + gt_ref[...] * acc_s[...].reshape(nb, tm, D_MODEL)
        if final:
            xn = _rms_rows(xn) * gf_ref[...]
        xo_ref[...] = xn


def _mlp_stream(x, mod, layer, w, g_final, *, final, name):
    nb, tm, _ = x.shape

    def mod_spec(j):
        return pl.BlockSpec((None, None, nb, 1, D_MODEL), lambda c: (layer, j, 0, 0, 0))

    in_specs = [
        pl.BlockSpec((nb, tm, D_MODEL), lambda c: (0, 0, 0)),
        mod_spec(3), mod_spec(4), mod_spec(5),
        pl.BlockSpec((None, 1, D_MODEL), lambda c: (layer, 0, 0)),
        pl.BlockSpec((None, D_MODEL, FF_CHUNK), lambda c: (layer, 0, c)),
        pl.BlockSpec((None, FF_CHUNK, D_MODEL), lambda c: (layer, c, 0)),
        pl.BlockSpec((1, D_MODEL), lambda c: (0, 0)),
    ]
    return pl.pallas_call(
        functools.partial(_mlp_stream_kernel, nb=nb, tm=tm, final=final),
        out_shape=jax.ShapeDtypeStruct(x.shape, F32), grid=(D_FF // FF_CHUNK,), in_specs=in_specs,
        out_specs=pl.BlockSpec((nb, tm, D_MODEL), lambda c: (0, 0, 0)),
        scratch_shapes=[pltpu.VMEM((nb * tm, D_MODEL), BF16), pltpu.VMEM((nb * tm, D_MODEL), F32)],
        compiler_params=pltpu.CompilerParams(dimension_semantics=("arbitrary",), vmem_limit_bytes=VMEM_LIMIT),
        name=name,
    )(x, mod, mod, mod, w["g2"], w["w_up"], w["w_down"], g_final)


def kernel(x_prompt, x_sample, c_prompt, c_sample, state_C, state_n, state_m, state_conv, w_ada, b_ada,
           g_norm1, w_in, b_in, conv_w, g_mix_out, w_out, g_norm2, w_up, w_down, g_final):
    bp, sp, _ = x_prompt.shape
    bs, ss, _ = x_sample.shape

    def relayout_bias(a):
        gi = a[..., REF_OFF_I:REF_OFF_F]
        gf = a[..., REF_OFF_F:REF_OFF_B]
        pad = [(0, 0)] * (a.ndim - 1) + [(0, LANES - M_HEADS)]
        return jnp.concatenate([a[..., :REF_OFF_I], a[..., REF_OFF_B:], jnp.pad(gi, pad), jnp.pad(gf, pad)], axis=-1)

    w = {
        "g1": g_norm1.reshape(DEPTH, 1, D_MODEL),
        "w_in": _relayout_w_in(w_in),
        "b_in": relayout_bias(b_in).reshape(DEPTH, 1, N_Z),
        "conv_w": conv_w,
        "g_mix": g_mix_out.reshape(DEPTH, 1, D_MODEL),
        "w_out": w_out.astype(BF16),
        "g2": g_norm2.reshape(DEPTH, 1, D_MODEL),
        "w_up": w_up.astype(BF16),
        "w_down": w_down.astype(BF16),
    }
    gfin = g_final.reshape(1, D_MODEL)

    mod = _modulation(jnp.concatenate([c_sample, c_prompt], axis=0), w_ada, b_ada)

    zero_state = (jnp.zeros((1, bp, M_HEADS, M_DK, M_DK), F32), jnp.zeros((1, bp, M_HEADS, M_DK), F32),
                  jnp.zeros((1, bp, 1, LANES), F32), jnp.zeros((1, bp, CONV_W - 1, S_WIDTH), F32))
    m_pad = jnp.pad(state_m, ((0, 0), (0, 0), (0, LANES - M_HEADS))).reshape(DEPTH, bs, 1, LANES)

    tm_p = 1024
    nb_s = 16
    xp, xs = x_prompt, x_sample
    outs = [[] for _ in range(8)]
    for l in range(DEPTH):
        last = l == DEPTH - 1
        xp, c1, n1, m1, cv1 = _mixer(xp, mod, bs, l, w, zero_state + (0,), nb=1, tm=tm_p, parts=2,
                                     name=f"mixer_prompt_{l}")
        xs, c2, n2, m2, cv2 = _mixer(xs, mod, 0, l, w, (state_C, state_n, m_pad, state_conv, l),
                                     nb=nb_s, tm=ss, parts=1, name=f"mixer_sample_{l}")
        for lst, val in zip(outs, (c1, n1, m1[:, 0, :M_HEADS], cv1, c2, n2, m2[:, 0, :M_HEADS], cv2)):
            lst.append(val)
        xp = _mlp(xp, mod, bs, l, w, gfin, nb=1, tm=tm_p, final=last, name=f"mlp_prompt_{l}")
        xs = _mlp_stream(xs, mod, l, w, gfin, final=last, name=f"mlp_sample_{l}")
    return (xp, xs) + tuple(jnp.stack(o) for o in outs)
```

```python
import functools
import types

import jax
import jax.numpy as jnp
from jax import lax
from jax.experimental import pallas as pl
from jax.experimental.pallas import tpu as pltpu

F32 = jnp.float32
BF16 = jnp.bfloat16

D_MODEL = 1024
DEPTH = 4
M_HEADS = 4
M_DK = 128
M_WIDTH = M_HEADS * M_DK
S_WIDTH = 512
S_GROUPS = 4
CONV_W = 3
D_FF = 4 * D_MODEL
N_MOD = 6
EPS = 1e-6

REF_OFF_I = 4 * M_WIDTH
REF_OFF_F = REF_OFF_I + M_HEADS
REF_OFF_B = REF_OFF_F + M_HEADS
REF_N_IN = REF_OFF_B + 3 * S_WIDTH

LANES = 128
BLK = 128
OFF_Q, OFF_K, OFF_V, OFF_O = 0, 512, 1024, 1536
OFF_B, OFF_C, OFF_X = 2048, 2560, 3072
OFF_GI, OFF_GF = 3584, 3712
N_Z = 3840
FF_CHUNK = 1024
RELAYOUT_COLS = 256
VMEM_LIMIT = 58 * 1024 * 1024


def _rms_rows(x):
    return x * lax.rsqrt(jnp.mean(x * x, axis=-1, keepdims=True) + EPS)


def _log_sigmoid(x):
    return jnp.minimum(x, 0.0) - jnp.log1p(jnp.exp(-jnp.abs(x)))


def _mod_kernel(c_ref, w_ref, b_ref, o_ref):
    c = c_ref[...]
    a = (c * jax.nn.sigmoid(c)).astype(BF16)
    o_ref[...] = jnp.dot(a, w_ref[...].astype(BF16), preferred_element_type=F32) + b_ref[...]


def _modulation(c_all, w_ada, b_ada):
    nrow = c_all.shape[0]
    b4 = b_ada.reshape(DEPTH, N_MOD, 1, D_MODEL)
    out = pl.pallas_call(
        _mod_kernel,
        out_shape=jax.ShapeDtypeStruct((DEPTH, N_MOD, nrow, D_MODEL), F32),
        grid=(DEPTH, N_MOD),
        in_specs=[
            pl.BlockSpec((nrow, D_MODEL), lambda l, j: (0, 0)),
            pl.BlockSpec((None, D_MODEL, D_MODEL), lambda l, j: (l, 0, j)),
            pl.BlockSpec((None, None, 1, D_MODEL), lambda l, j: (l, j, 0, 0)),
        ],
        out_specs=pl.BlockSpec((None, None, nrow, D_MODEL), lambda l, j: (l, j, 0, 0)),
        compiler_params=pltpu.CompilerParams(dimension_semantics=("arbitrary", "arbitrary")),
        name="adaln_modulation",
    )(c_all, w_ada, b4)
    return out.reshape(DEPTH, N_MOD, nrow, 1, D_MODEL)


def _relayout_kernel(wt_ref, o_ref):
    def put(col0, row0, ncol):
        for c in range(0, ncol, RELAYOUT_COLS):
            o_ref[:, col0 + c:col0 + c + RELAYOUT_COLS] = (
                wt_ref[row0 + c:row0 + c + RELAYOUT_COLS, :].T.astype(BF16))

    put(0, 0, REF_OFF_I)
    put(OFF_B, REF_OFF_B, REF_N_IN - REF_OFF_B)
    g = wt_ref[REF_OFF_I:REF_OFF_I + LANES, :].T
    lane = lax.broadcasted_iota(jnp.int32, g.shape, 1)
    o_ref[:, OFF_GI:OFF_GF] = jnp.where(lane < M_HEADS, g, 0.0).astype(BF16)
    o_ref[:, OFF_GF:N_Z] = jnp.where(lane < M_HEADS, pltpu.roll(g, LANES - M_HEADS, axis=1), 0.0).astype(BF16)


def _relayout_w_in(w_in):
    wt = jnp.swapaxes(w_in, 1, 2)
    return pl.pallas_call(
        _relayout_kernel,
        out_shape=jax.ShapeDtypeStruct((DEPTH, D_MODEL, N_Z), BF16),
        grid=(DEPTH,),
        in_specs=[pl.BlockSpec((None, REF_N_IN, D_MODEL), lambda l: (l, 0, 0))],
        out_specs=pl.BlockSpec((None, D_MODEL, N_Z), lambda l: (l, 0, 0)),
        compiler_params=pltpu.CompilerParams(dimension_semantics=("arbitrary",), vmem_limit_bytes=VMEM_LIMIT),
        name="relayout_w_in",
    )(wt)


def _mixer_part(r, *, nb, tm, seg, first_tile, lo, th):
    assert th == tm or nb == 1
    rows = nb * th
    fsl = slice(lo, lo + rows)
    tsl = slice(lo, lo + th)
    nblk = rows // BLK
    blk0 = lo // BLK
    nseg = BLK // seg
    v = types.SimpleNamespace(blk={})

    if first_tile is None:
        assert lo == 0
        c_in, n_in, m_in, cv_in = r.c0, r.n0, r.m0, r.cv0
    else:
        c_in, n_in, m_in, cv_in = r.c, r.n, r.m, r.cv

    def proj(off, width):
        return (jnp.dot(v.hb, r.win[:, off:off + width], preferred_element_type=F32)
                + r.bin[:, off:off + width])

    def norm_gates():
        if first_tile is not None and lo == 0:
            @pl.when(first_tile)
            def _():
                r.c[...] = r.c0[...]
                r.n[...] = r.n0[...]
                r.m[...] = r.m0[...]
                r.cv[...] = r.cv0[...]

        v.x3 = r.x[:, tsl, :]
        h3 = _rms_rows(v.x3) * (r.g1[...] * (1.0 + r.sc[...])) + r.sh[...]
        v.hb = h3.reshape(rows, D_MODEL).astype(BF16)
        v.gates = proj(OFF_GI, 2 * LANES)

    def q_proj():
        r.q_s[fsl, :] = proj(OFF_Q, M_WIDTH).astype(BF16)

    def gate_scalars():
        logi = v.gates[:, :LANES]
        logf = _log_sigmoid(v.gates[:, LANES:])
        rin = lax.broadcasted_iota(jnp.int32, (rows, LANES), 0) & (th - 1)
        bt = logf
        shift = 1
        while shift < th:
            bt = bt + jnp.where(rin >= shift, pltpu.roll(bt, shift, axis=0), 0.0)
            shift *= 2
        rr = logi - bt
        cm = rr
        shift = 1
        while shift < th:
            cm = jnp.maximum(cm, jnp.where(rin >= shift, pltpu.roll(cm, shift, axis=0), -jnp.inf))
            shift *= 2
        m0b = jnp.broadcast_to(m_in[...], (nb, th, LANES)).reshape(rows, LANES)
        cmx = jnp.maximum(m0b, cm)
        p = -cmx
        m = bt + cmx
        r.m[...] = m.reshape(nb, th, LANES)[:, th - 1:th, :]
        nchunk = rows // seg
        p3 = p.reshape(nchunk, seg, LANES)
        p_before = jnp.where(rin == 0, -m0b, pltpu.roll(p, 1, axis=0)).reshape(nchunk, seg, LANES)[:, 0:1, :]
        r.p_s[fsl, :] = p
        r.g_s[fsl, :] = jnp.exp(p3 - p_before).reshape(rows, LANES)
        r.we_s[fsl, :] = jnp.exp(p3[:, seg - 1:seg, :] + rr.reshape(nchunk, seg, LANES)).reshape(rows, LANES)
        r.em_s[fsl, :] = jnp.exp(-m)
        for bi in range(nblk):
            r.rt_s[:, lo + bi * BLK:lo + (bi + 1) * BLK] = rr[bi * BLK:(bi + 1) * BLK, :].T

    def k_proj():
        r.k_s[fsl, :] = (proj(OFF_K, M_WIDTH) * (M_DK ** -0.5)).astype(BF16)

    def v_proj():
        r.v_s[fsl, :] = proj(OFF_V, M_WIDTH).astype(BF16)

    def o_proj():
        r.og_s[fsl, :] = jax.nn.sigmoid(proj(OFF_O, M_WIDTH))

    def block_diag(a, rowseg):
        if nseg == 1:
            return a
        zero = jnp.zeros_like(a)
        return jnp.concatenate([jnp.where(rowseg == j, a, zero) for j in range(nseg)], axis=1)

    def block_consts(bi):
        ri = lax.broadcasted_iota(jnp.int32, (BLK, BLK), 0)
        ci = lax.broadcasted_iota(jnp.int32, (BLK, BLK), 1)
        sh = seg.bit_length() - 1
        rowseg = ri >> sh
        mask = ci <= ri
        if nseg > 1:
            mask = mask & (rowseg == (ci >> sh))
        r0 = bi * BLK
        rsl = slice(r0, r0 + BLK)
        seq0 = r0 // tm
        return rowseg, mask, rsl, slice(seq0, seq0 + nseg)

    def block_scores(bi):
        rowseg, mask, rsl, _ = block_consts(bi)
        hsl = [slice(h * M_DK, (h + 1) * M_DK) for h in range(M_HEADS)]
        q = [r.q_s[rsl, hs] for hs in hsl]
        k = [r.k_s[rsl, hs] for hs in hsl]
        vv = [r.v_s[rsl, hs] for hs in hsl]
        s = [lax.dot_general(q[h], k[h], (((1,), (1,)), ((), ())), preferred_element_type=F32)
             for h in range(M_HEADS)]
        sw = [s[h] * jnp.exp(jnp.where(mask, r.p_s[rsl, h:h + 1] + r.rt_s[h:h + 1, rsl], -jnp.inf))
              for h in range(M_HEADS)]
        kw = [k[h].astype(F32) * r.we_s[rsl, h:h + 1] for h in range(M_HEADS)]
        blk = types.SimpleNamespace(q=q)
        blk.rowsum = [jnp.sum(sw[h], axis=-1, keepdims=True) for h in range(M_HEADS)]
        blk.intra = [jnp.dot(sw[h].astype(BF16), vv[h], preferred_element_type=F32) for h in range(M_HEADS)]
        blk.upd = [lax.dot_general(block_diag(kw[h].astype(BF16), rowseg), vv[h], (((0,), (0,)), ((), ())),
                                   preferred_element_type=F32) for h in range(M_HEADS)]
        blk.ksum = [jnp.sum(kw[h].reshape(nseg, seg, M_DK), axis=1, keepdims=True) for h in range(M_HEADS)]
        v.blk[bi] = blk

    def block_state(bi):
        rowseg, _, rsl, ssl = block_consts(bi)
        blk = v.blk.pop(bi)
        g_blk = r.g_s[rsl, :].reshape(nseg, seg, LANES)
        heads = range(M_HEADS)
        c0 = [c_in[ssl, h] for h in heads]
        n0 = [n_in[ssl, h:h + 1, :] for h in heads]
        inter = [jnp.dot(block_diag(blk.q[h], rowseg), c0[h].astype(BF16).reshape(nseg * M_DK, M_DK),
                         preferred_element_type=F32) for h in heads]
        qn = [jnp.sum(blk.q[h].astype(F32) * jnp.broadcast_to(n0[h], (nseg, seg, M_DK)).reshape(BLK, M_DK),
                      axis=-1, keepdims=True) for h in heads]
        gc = [r.g_s[rsl, h:h + 1] for h in heads]
        num = [gc[h] * inter[h] + blk.intra[h] for h in heads]
        den = [gc[h] * qn[h] + blk.rowsum[h] for h in heads]
        hh = [num[h] * (1.0 / jnp.maximum(jnp.abs(den[h]), r.em_s[rsl, h:h + 1])) for h in heads]
        for h in heads:
            r.hn_s[rsl, h * M_DK:(h + 1) * M_DK] = _rms_rows(hh[h])
        for h in heads:
            decay = g_blk[:, seg - 1:seg, h:h + 1]
            r.c[ssl, h] = decay * c0[h] + blk.upd[h].reshape(nseg, M_DK, M_DK)
            r.n[ssl, h:h + 1, :] = decay * n0[h] + blk.ksum[h]

    def conv_inputs():
        if lo == 0:
            r.ubuf_s[:, 6:8, :] = cv_in[...]
        u = proj(OFF_C, S_WIDTH) * proj(OFF_X, S_WIDTH)
        r.ubuf_s[:, 8 + lo:8 + lo + th, :] = u.reshape(nb, th, S_WIDTH)
        r.cv[...] = r.ubuf_s[:, lo + th + 6:lo + th + 8, :]

    def conv_outputs():
        bg = proj(OFF_B, S_WIDTH)
        cw = r.cw[...]
        yc = (cw[0:1] * r.ubuf_s[:, 6 + lo:6 + lo + th, :] + cw[1:2] * r.ubuf_s[:, 7 + lo:7 + lo + th, :]
              + cw[2:3] * r.ubuf_s[:, 8 + lo:8 + lo + th, :])
        ysv = (bg.reshape(nb, th, S_WIDTH) * yc).reshape(rows, S_WIDTH)
        for grp in range(S_GROUPS):
            cs = slice(M_WIDTH + grp * LANES, M_WIDTH + (grp + 1) * LANES)
            r.mix_s[fsl, cs] = (_rms_rows(ysv[:, grp * LANES:(grp + 1) * LANES]) * r.gmix[:, cs]).astype(BF16)

    def conv_half_out():
        v.y_conv = jnp.dot(r.mix_s[fsl, M_WIDTH:], r.wout[M_WIDTH:, :], preferred_element_type=F32)

    def finish():
        r.mix_s[fsl, :M_WIDTH] = (r.hn_s[fsl, :] * r.og_s[fsl, :] * r.gmix[:, :M_WIDTH]).astype(BF16)
        y = v.y_conv + jnp.dot(r.mix_s[fsl, :M_WIDTH], r.wout[:M_WIDTH, :], preferred_element_type=F32)
        r.xo[:, tsl, :] = v.x3 + r.gt[...] * y.reshape(nb, th, D_MODEL)

    blocks = [functools.partial(block_scores, blk0)]
    for bi in range(blk0, blk0 + nblk):
        if bi + 1 < blk0 + nblk:
            blocks.append(functools.partial(block_scores, bi + 1))
        blocks.append(functools.partial(block_state, bi))
    return types.SimpleNamespace(
        pre=[norm_gates, q_proj, gate_scalars, k_proj, v_proj], blocks=blocks,
        fillers=[o_proj, conv_inputs, conv_outputs, conv_half_out], finish=finish)


def _spread(chain, others):
    out, done = [], 0
    for i, step in enumerate(chain):
        out.append(step)
        want = ((i + 1) * len(others)) // len(chain)
        out.extend(others[done:want])
        done = want
    return out


def _mixer_steps(r, *, nb, tm, seg, first_tile, parts):
    th = tm // parts
    part = [_mixer_part(r, nb=nb, tm=tm, seg=seg, first_tile=first_tile, lo=i * th, th=th) for i in range(parts)]
    if parts == 1:
        a = part[0]
        return a.pre + _spread(a.blocks, a.fillers) + [a.finish]
    a, b = part
    return (a.pre
            + _spread(a.blocks, b.pre + a.fillers[:2])
            + _spread(b.blocks, a.fillers[2:] + [a.finish] + b.fillers[:2])
            + b.fillers[2:] + [b.finish])


_MIXER_REFS = (
    "x", "sh", "sc", "gt", "g1", "win", "bin", "cw", "gmix", "wout", "c0", "n0", "m0", "cv0",
    "xo", "c", "n", "m", "cv",
    "q_s", "k_s", "v_s", "og_s", "hn_s", "mix_s", "ubuf_s", "p_s", "g_s", "we_s", "em_s", "rt_s",
)


def _mixer_kernel(*refs, nb, tm, seg, nt, parts, n_extra):
    n_in = _MIXER_REFS.index("xo")
    refs = refs[:n_in] + refs[n_in + n_extra:]
    r = types.SimpleNamespace(**dict(zip(_MIXER_REFS, refs, strict=True)))
    first_tile = None if nt == 1 else pl.program_id(1) == 0
    for step in _mixer_steps(r, nb=nb, tm=tm, seg=seg, first_tile=first_tile, parts=parts):
        step()


def _mixer(x, mod, mod_row0, layer, w, state, *, nb, tm, parts, name, c_stack=None, stack_c=False):
    bsz, t_len, _ = x.shape
    rows = nb * tm
    seg = min(BLK, tm // parts)
    assert (rows // parts) % BLK == 0 and bsz % nb == 0 and t_len % tm == 0 and mod_row0 % nb == 0
    assert tm & (tm - 1) == 0 and (nb == 1 or tm == t_len)
    c0, n0, m0, cv0, st_layer = state
    grid = (bsz // nb, t_len // tm)
    mrow = mod_row0 // nb

    def mod_spec(j):
        return pl.BlockSpec((None, None, nb, 1, D_MODEL), lambda b, t: (layer, j, mrow + b, 0, 0))

    def wspec(shape, resident=False):
        nd = len(shape)
        mode = {"pipeline_mode": pl.Buffered(1)} if resident else {}
        return pl.BlockSpec((None,) + shape, lambda b, t: (layer,) + (0,) * nd, **mode)

    def sspec(shape):
        nd = len(shape)
        return pl.BlockSpec((None, nb) + shape, lambda b, t: (st_layer, b) + (0,) * nd)

    def ospec(shape):
        nd = len(shape)
        return pl.BlockSpec((nb,) + shape, lambda b, t: (b,) + (0,) * nd)

    state_shapes = ((M_HEADS, M_DK, M_DK), (M_HEADS, M_DK), (1, LANES), (CONV_W - 1, S_WIDTH))
    in_specs = [
        pl.BlockSpec((nb, tm, D_MODEL), lambda b, t: (b, t, 0)),
        mod_spec(0), mod_spec(1), mod_spec(2),
        wspec((1, D_MODEL)),
        wspec((D_MODEL, N_Z), True),
        wspec((1, N_Z)),
        wspec((CONV_W, S_WIDTH)),
        wspec((1, D_MODEL)),
        wspec((D_MODEL, D_MODEL), True),
    ] + [sspec(s) for s in state_shapes]
    out_shape = [jax.ShapeDtypeStruct(x.shape, F32)] + [jax.ShapeDtypeStruct((bsz,) + s, F32) for s in state_shapes]
    out_specs = [pl.BlockSpec((nb, tm, D_MODEL), lambda b, t: (b, t, 0))] + [ospec(s) for s in state_shapes]
    extra, aliases = [], {}
    if stack_c:
        out_shape[1] = jax.ShapeDtypeStruct((DEPTH, bsz) + state_shapes[0], F32)
        out_specs[1] = pl.BlockSpec((None, nb) + state_shapes[0], lambda b, t: (layer, b, 0, 0, 0))
        if c_stack is not None:
            extra = [c_stack]
            in_specs.append(pl.BlockSpec(memory_space=pl.ANY))
            aliases = {len(in_specs) - 1: 1}
    scratch = [
        pltpu.VMEM((rows, M_WIDTH), BF16),
        pltpu.VMEM((rows, M_WIDTH), BF16),
        pltpu.VMEM((rows, M_WIDTH), BF16),
        pltpu.VMEM((rows, M_WIDTH), F32),
        pltpu.VMEM((rows, M_WIDTH), F32),
        pltpu.VMEM((rows, D_MODEL), BF16),
        pltpu.VMEM((nb, tm + 8, S_WIDTH), F32),
        pltpu.VMEM((rows, LANES), F32),
        pltpu.VMEM((rows, LANES), F32),
        pltpu.VMEM((rows, LANES), F32),
        pltpu.VMEM((rows, LANES), F32),
        pltpu.VMEM((LANES, rows), F32),
    ]
    kern = functools.partial(_mixer_kernel, nb=nb, tm=tm, seg=seg, nt=t_len // tm, parts=parts, n_extra=len(extra))
    return pl.pallas_call(
        kern, out_shape=tuple(out_shape), grid=grid, in_specs=in_specs, out_specs=tuple(out_specs),
        scratch_shapes=scratch, input_output_aliases=aliases,
        compiler_params=pltpu.CompilerParams(dimension_semantics=("arbitrary", "arbitrary"),
                                             vmem_limit_bytes=VMEM_LIMIT),
        name=name,
    )(x, mod, mod, mod, w["g1"], w["w_in"], w["b_in"], w["conv_w"], w["g_mix"], w["w_out"],
      c0, n0, m0, cv0, *extra)


def _mlp_kernel(x_ref, sh_ref, sc_ref, gt_ref, g2_ref, wup_ref, wdn_ref, gf_ref, xo_ref, *, nb, tm, final):
    rows = nb * tm
    x3 = x_ref[...]
    h3 = _rms_rows(x3) * (g2_ref[...] * (1.0 + sc_ref[...])) + sh_ref[...]
    hb = h3.reshape(rows, D_MODEL).astype(BF16)
    acc = jnp.zeros((rows, D_MODEL), F32)
    for c in range(D_FF // FF_CHUNK):
        lo = c * FF_CHUNK
        up = jnp.dot(hb, wup_ref[:, lo:lo + FF_CHUNK], preferred_element_type=F32)
        act = jnp.square(jnp.maximum(up, 0.0)).astype(BF16)
        acc = acc + jnp.dot(act, wdn_ref[lo:lo + FF_CHUNK, :], preferred_element_type=F32)
    xn = x3 + gt_ref[...] * acc.reshape(nb, tm, D_MODEL)
    if final:
        xn = _rms_rows(xn) * gf_ref[...]
    xo_ref[...] = xn


def _mlp(x, mod, mod_row0, layer, w, g_final, *, nb, tm, final, name):
    bsz, t_len, _ = x.shape
    assert bsz % nb == 0 and t_len % tm == 0 and mod_row0 % nb == 0
    grid = (bsz // nb, t_len // tm)
    mrow = mod_row0 // nb

    def mod_spec(j):
        return pl.BlockSpec((None, None, nb, 1, D_MODEL), lambda b, t: (layer, j, mrow + b, 0, 0))

    in_specs = [
        pl.BlockSpec((nb, tm, D_MODEL), lambda b, t: (b, t, 0)),
        mod_spec(3), mod_spec(4), mod_spec(5),
        pl.BlockSpec((None, 1, D_MODEL), lambda b, t: (layer, 0, 0)),
        pl.BlockSpec((None, D_MODEL, D_FF), lambda b, t: (layer, 0, 0), pipeline_mode=pl.Buffered(1)),
        pl.BlockSpec((None, D_FF, D_MODEL), lambda b, t: (layer, 0, 0), pipeline_mode=pl.Buffered(1)),
        pl.BlockSpec((1, D_MODEL), lambda b, t: (0, 0)),
    ]
    kern = functools.partial(_mlp_kernel, nb=nb, tm=tm, final=final)
    return pl.pallas_call(
        kern, out_shape=jax.ShapeDtypeStruct(x.shape, F32), grid=grid, in_specs=in_specs,
        out_specs=pl.BlockSpec((nb, tm, D_MODEL), lambda b, t: (b, t, 0)),
        compiler_params=pltpu.CompilerParams(dimension_semantics=("arbitrary", "arbitrary"),
                                             vmem_limit_bytes=VMEM_LIMIT),
        name=name,
    )(x, mod, mod, mod, w["g2"], w["w_up"], w["w_down"], g_final)


def _mlp_stream_kernel(x_ref, sh_ref, sc_ref, gt_ref, g2_ref, wup_ref, wdn_ref, gf_ref, xo_ref, hb_s, acc_s,
                       *, nb, tm, final):
    rows = nb * tm
    c = pl.program_id(0)

    @pl.when(c == 0)
    def _():
        h3 = _rms_rows(x_ref[...]) * (g2_ref[...] * (1.0 + sc_ref[...])) + sh_ref[...]
        hb_s[...] = h3.reshape(rows, D_MODEL).astype(BF16)
        acc_s[...] = jnp.zeros((rows, D_MODEL), F32)

    up = jnp.dot(hb_s[...], wup_ref[...], preferred_element_type=F32)
    act = jnp.square(jnp.maximum(up, 0.0)).astype(BF16)
    acc_s[...] += jnp.dot(act, wdn_ref[...], preferred_element_type=F32)

    @pl.when(c == pl.num_programs(0) - 1)
    def _():
        xn = x_ref[...] + gt_ref[...] * acc_s[...].reshape(nb, tm, D_MODEL)
        if final:
            xn = _rms_rows(xn) * gf_ref[...]
        xo_ref[...] = xn


def _mlp_stream(x, mod, layer, w, g_final, *, final, name):
    nb, tm, _ = x.shape

    def mod_spec(j):
        return pl.BlockSpec((None, None, nb, 1, D_MODEL), lambda c: (layer, j, 0, 0, 0))

    in_specs = [
        pl.BlockSpec((nb, tm, D_MODEL), lambda c: (0, 0, 0)),
        mod_spec(3), mod_spec(4), mod_spec(5),
        pl.BlockSpec((None, 1, D_MODEL), lambda c: (layer, 0, 0)),
        pl.BlockSpec((None, D_MODEL, FF_CHUNK), lambda c: (layer, 0, c)),
        pl.BlockSpec((None, FF_CHUNK, D_MODEL), lambda c: (layer, c, 0)),
        pl.BlockSpec((1, D_MODEL), lambda c: (0, 0)),
    ]
    return pl.pallas_call(
        functools.partial(_mlp_stream_kernel, nb=nb, tm=tm, final=final),
        out_shape=jax.ShapeDtypeStruct(x.shape, F32), grid=(D_FF // FF_CHUNK,), in_specs=in_specs,
        out_specs=pl.BlockSpec((nb, tm, D_MODEL), lambda c: (0, 0, 0)),
        scratch_shapes=[pltpu.VMEM((nb * tm, D_MODEL), BF16), pltpu.VMEM((nb * tm, D_MODEL), F32)],
        compiler_params=pltpu.CompilerParams(dimension_semantics=("arbitrary",), vmem_limit_bytes=VMEM_LIMIT),
        name=name,
    )(x, mod, mod, mod, w["g2"], w["w_up"], w["w_down"], g_final)


def kernel(x_prompt, x_sample, c_prompt, c_sample, state_C, state_n, state_m, state_conv, w_ada, b_ada,
           g_norm1, w_in, b_in, conv_w, g_mix_out, w_out, g_norm2, w_up, w_down, g_final):
    bp, sp, _ = x_prompt.shape
    bs, ss, _ = x_sample.shape

    def relayout_bias(a):
        gi = a[..., REF_OFF_I:REF_OFF_F]
        gf = a[..., REF_OFF_F:REF_OFF_B]
        pad = [(0, 0)] * (a.ndim - 1) + [(0, LANES - M_HEADS)]
        return jnp.concatenate([a[..., :REF_OFF_I], a[..., REF_OFF_B:], jnp.pad(gi, pad), jnp.pad(gf, pad)], axis=-1)

    w = {
        "g1": g_norm1.reshape(DEPTH, 1, D_MODEL),
        "w_in": _relayout_w_in(w_in),
        "b_in": relayout_bias(b_in).reshape(DEPTH, 1, N_Z),
        "conv_w": conv_w,
        "g_mix": g_mix_out.reshape(DEPTH, 1, D_MODEL),
        "w_out": w_out,
        "g2": g_norm2.reshape(DEPTH, 1, D_MODEL),
        "w_up": w_up,
        "w_down": w_down,
    }
    gfin = g_final.reshape(1, D_MODEL)

    mod = _modulation(jnp.concatenate([c_sample, c_prompt], axis=0), w_ada, b_ada)

    zero_state = (jnp.zeros((1, bp, M_HEADS, M_DK, M_DK), F32), jnp.zeros((1, bp, M_HEADS, M_DK), F32),
                  jnp.zeros((1, bp, 1, LANES), F32), jnp.zeros((1, bp, CONV_W - 1, S_WIDTH), F32))
    m_pad = jnp.pad(state_m, ((0, 0), (0, 0), (0, LANES - M_HEADS))).reshape(DEPTH, bs, 1, LANES)

    tm_p = 1024
    nb_s = 16
    xp, xs = x_prompt, x_sample
    outs = [[] for _ in range(7)]
    s_c = None
    for l in range(DEPTH):
        last = l == DEPTH - 1
        xp, c1, n1, m1, cv1 = _mixer(xp, mod, bs, l, w, zero_state + (0,), nb=1, tm=tm_p, parts=2,
                                     name=f"mixer_prompt_{l}")
        xs, s_c, n2, m2, cv2 = _mixer(xs, mod, 0, l, w, (state_C, state_n, m_pad, state_conv, l),
                                      nb=nb_s, tm=ss, parts=1, name=f"mixer_sample_{l}", c_stack=s_c, stack_c=True)
        for lst, val in zip(outs, (c1, n1, m1[:, 0, :M_HEADS], cv1, n2, m2[:, 0, :M_HEADS], cv2)):
            lst.append(val)
        xp = _mlp(xp, mod, bs, l, w, gfin, nb=1, tm=tm_p, final=last, name=f"mlp_prompt_{l}")
        xs = _mlp_stream(xs, mod, l, w, gfin, final=last, name=f"mlp_sample_{l}")
    p_c, p_n, p_m, p_cv, s_n, s_m, s_cv = (jnp.stack(o) for o in outs)
    return (xp, xs, p_c, p_n, p_m, p_cv, s_c, s_n, s_m, s_cv)
```

```python
import functools
import types

import jax
import jax.numpy as jnp
from jax import lax
from jax.experimental import pallas as pl
from jax.experimental.pallas import tpu as pltpu

F32 = jnp.float32
BF16 = jnp.bfloat16

D_MODEL = 1024
DEPTH = 4
M_HEADS = 4
M_DK = 128
M_WIDTH = M_HEADS * M_DK
S_WIDTH = 512
S_GROUPS = 4
CONV_W = 3
D_FF = 4 * D_MODEL
N_MOD = 6
EPS = 1e-6

REF_OFF_I = 4 * M_WIDTH
REF_OFF_F = REF_OFF_I + M_HEADS
REF_OFF_B = REF_OFF_F + M_HEADS
REF_N_IN = REF_OFF_B + 3 * S_WIDTH

LANES = 128
BLK = 128
OFF_Q, OFF_K, OFF_V, OFF_O = 0, 512, 1024, 1536
OFF_B, OFF_C, OFF_X = 2048, 2560, 3072
OFF_GI, OFF_GF = 3584, 3712
N_Z = 3840
FF_CHUNK = 1024
RELAYOUT_COLS = 256
VMEM_LIMIT = 58 * 1024 * 1024


def _rms_rows(x):
    return x * lax.rsqrt(jnp.mean(x * x, axis=-1, keepdims=True) + EPS)


def _log_sigmoid(x):
    return jnp.minimum(x, 0.0) - jnp.log1p(jnp.exp(-jnp.abs(x)))


MOD_PER_STEP = 2


def _mod_kernel(c_ref, w_ref, b_ref, o_ref):
    c = c_ref[...]
    a = (c * jax.nn.sigmoid(c)).astype(BF16)
    res = jnp.dot(a, w_ref[...].astype(BF16), preferred_element_type=F32)
    for k in range(MOD_PER_STEP):
        o_ref[k] = res[:, k * D_MODEL:(k + 1) * D_MODEL] + b_ref[k]


def _modulation(c_all, w_ada, b_ada):
    nrow = c_all.shape[0]
    b4 = b_ada.reshape(DEPTH, N_MOD, 1, D_MODEL)
    out = pl.pallas_call(
        _mod_kernel,
        out_shape=jax.ShapeDtypeStruct((DEPTH, N_MOD, nrow, D_MODEL), F32),
        grid=(DEPTH, N_MOD // MOD_PER_STEP),
        in_specs=[
            pl.BlockSpec((nrow, D_MODEL), lambda l, j: (0, 0)),
            pl.BlockSpec((None, D_MODEL, MOD_PER_STEP * D_MODEL), lambda l, j: (l, 0, j)),
            pl.BlockSpec((None, MOD_PER_STEP, 1, D_MODEL), lambda l, j: (l, j, 0, 0)),
        ],
        out_specs=pl.BlockSpec((None, MOD_PER_STEP, nrow, D_MODEL), lambda l, j: (l, j, 0, 0)),
        compiler_params=pltpu.CompilerParams(dimension_semantics=("arbitrary", "arbitrary"),
                                             vmem_limit_bytes=VMEM_LIMIT),
        name="adaln_modulation",
    )(c_all, w_ada, b4)
    return out.reshape(DEPTH, N_MOD, nrow, 1, D_MODEL)


def _relayout_kernel(wt_ref, o_ref):
    def put(col0, row0, ncol):
        for c in range(0, ncol, RELAYOUT_COLS):
            o_ref[:, col0 + c:col0 + c + RELAYOUT_COLS] = (
                wt_ref[row0 + c:row0 + c + RELAYOUT_COLS, :].T.astype(BF16))

    put(0, 0, REF_OFF_I)
    put(OFF_B, REF_OFF_B, REF_N_IN - REF_OFF_B)
    g = wt_ref[REF_OFF_I:REF_OFF_I + LANES, :].T
    lane = lax.broadcasted_iota(jnp.int32, g.shape, 1)
    o_ref[:, OFF_GI:OFF_GF] = jnp.where(lane < M_HEADS, g, 0.0).astype(BF16)
    o_ref[:, OFF_GF:N_Z] = jnp.where(lane < M_HEADS, pltpu.roll(g, LANES - M_HEADS, axis=1), 0.0).astype(BF16)


def _relayout_w_in(w_in):
    wt = jnp.swapaxes(w_in, 1, 2)
    return pl.pallas_call(
        _relayout_kernel,
        out_shape=jax.ShapeDtypeStruct((DEPTH, D_MODEL, N_Z), BF16),
        grid=(DEPTH,),
        in_specs=[pl.BlockSpec((None, REF_N_IN, D_MODEL), lambda l: (l, 0, 0))],
        out_specs=pl.BlockSpec((None, D_MODEL, N_Z), lambda l: (l, 0, 0)),
        compiler_params=pltpu.CompilerParams(dimension_semantics=("arbitrary",), vmem_limit_bytes=VMEM_LIMIT),
        name="relayout_w_in",
    )(wt)


def _mixer_part(r, *, nb, tm, seg, first_tile, lo, th):
    assert th == tm or nb == 1
    rows = nb * th
    fsl = slice(lo, lo + rows)
    tsl = slice(lo, lo + th)
    nblk = rows // BLK
    blk0 = lo // BLK
    nseg = BLK // seg
    v = types.SimpleNamespace(blk={})

    if first_tile is None:
        assert lo == 0
        c_in, n_in, m_in, cv_in = r.c0, r.n0, r.m0, r.cv0
    else:
        c_in, n_in, m_in, cv_in = r.c, r.n, r.m, r.cv

    def proj(off, width):
        return (jnp.dot(v.hb, r.win[:, off:off + width], preferred_element_type=F32)
                + r.bin[:, off:off + width])

    def norm_gates():
        if first_tile is not None and lo == 0:
            @pl.when(first_tile)
            def _():
                r.c[...] = r.c0[...]
                r.n[...] = r.n0[...]
                r.m[...] = r.m0[...]
                r.cv[...] = r.cv0[...]

        v.x3 = r.x[:, tsl, :]
        h3 = _rms_rows(v.x3) * (r.g1[...] * (1.0 + r.sc[...])) + r.sh[...]
        v.hb = h3.reshape(rows, D_MODEL).astype(BF16)
        v.gates = proj(OFF_GI, 2 * LANES)

    def q_proj():
        r.q_s[fsl, :] = proj(OFF_Q, M_WIDTH).astype(BF16)

    def gate_scalars():
        logi = v.gates[:, :LANES]
        logf = _log_sigmoid(v.gates[:, LANES:])
        rin = lax.broadcasted_iota(jnp.int32, (rows, LANES), 0) & (th - 1)
        bt = logf
        shift = 1
        while shift < th:
            bt = bt + jnp.where(rin >= shift, pltpu.roll(bt, shift, axis=0), 0.0)
            shift *= 2
        rr = logi - bt
        cm = rr
        shift = 1
        while shift < th:
            cm = jnp.maximum(cm, jnp.where(rin >= shift, pltpu.roll(cm, shift, axis=0), -jnp.inf))
            shift *= 2
        m0b = jnp.broadcast_to(m_in[...], (nb, th, LANES)).reshape(rows, LANES)
        cmx = jnp.maximum(m0b, cm)
        p = -cmx
        m = bt + cmx
        r.m[...] = m.reshape(nb, th, LANES)[:, th - 1:th, :]
        nchunk = rows // seg
        p3 = p.reshape(nchunk, seg, LANES)
        p_before = jnp.where(rin == 0, -m0b, pltpu.roll(p, 1, axis=0)).reshape(nchunk, seg, LANES)[:, 0:1, :]
        r.p_s[fsl, :] = p
        r.g_s[fsl, :] = jnp.exp(p3 - p_before).reshape(rows, LANES)
        r.we_s[fsl, :] = jnp.exp(p3[:, seg - 1:seg, :] + rr.reshape(nchunk, seg, LANES)).reshape(rows, LANES)
        r.em_s[fsl, :] = jnp.exp(-m)
        for bi in range(nblk):
            r.rt_s[:, lo + bi * BLK:lo + (bi + 1) * BLK] = rr[bi * BLK:(bi + 1) * BLK, :].T

    def k_proj():
        r.k_s[fsl, :] = (proj(OFF_K, M_WIDTH) * (M_DK ** -0.5)).astype(BF16)

    def v_proj():
        r.v_s[fsl, :] = proj(OFF_V, M_WIDTH).astype(BF16)

    def o_proj():
        r.og_s[fsl, :] = jax.nn.sigmoid(proj(OFF_O, M_WIDTH))

    def block_diag(a, rowseg):
        if nseg == 1:
            return a
        zero = jnp.zeros_like(a)
        return jnp.concatenate([jnp.where(rowseg == j, a, zero) for j in range(nseg)], axis=1)

    def block_consts(bi):
        ri = lax.broadcasted_iota(jnp.int32, (BLK, BLK), 0)
        ci = lax.broadcasted_iota(jnp.int32, (BLK, BLK), 1)
        sh = seg.bit_length() - 1
        rowseg = ri >> sh
        mask = ci <= ri
        if nseg > 1:
            mask = mask & (rowseg == (ci >> sh))
        r0 = bi * BLK
        rsl = slice(r0, r0 + BLK)
        seq0 = r0 // tm
        return rowseg, mask, rsl, slice(seq0, seq0 + nseg)

    def block_scores(bi):
        rowseg, mask, rsl, _ = block_consts(bi)
        hsl = [slice(h * M_DK, (h + 1) * M_DK) for h in range(M_HEADS)]
        q = [r.q_s[rsl, hs] for hs in hsl]
        k = [r.k_s[rsl, hs] for hs in hsl]
        vv = [r.v_s[rsl, hs] for hs in hsl]
        s = [lax.dot_general(q[h], k[h], (((1,), (1,)), ((), ())), preferred_element_type=F32)
             for h in range(M_HEADS)]
        sw = [s[h] * jnp.exp(jnp.where(mask, r.p_s[rsl, h:h + 1] + r.rt_s[h:h + 1, rsl], -jnp.inf))
              for h in range(M_HEADS)]
        kw = [k[h].astype(F32) * r.we_s[rsl, h:h + 1] for h in range(M_HEADS)]
        blk = types.SimpleNamespace(q=q)
        blk.rowsum = [jnp.sum(sw[h], axis=-1, keepdims=True) for h in range(M_HEADS)]
        blk.intra = [jnp.dot(sw[h].astype(BF16), vv[h], preferred_element_type=F32) for h in range(M_HEADS)]
        blk.upd = [lax.dot_general(block_diag(kw[h].astype(BF16), rowseg), vv[h], (((0,), (0,)), ((), ())),
                                   preferred_element_type=F32) for h in range(M_HEADS)]
        blk.ksum = [jnp.sum(kw[h].reshape(nseg, seg, M_DK), axis=1, keepdims=True) for h in range(M_HEADS)]
        v.blk[bi] = blk

    def block_state(bi):
        rowseg, _, rsl, ssl = block_consts(bi)
        blk = v.blk.pop(bi)
        g_blk = r.g_s[rsl, :].reshape(nseg, seg, LANES)
        heads = range(M_HEADS)
        c0 = [c_in[ssl, h] for h in heads]
        n0 = [n_in[ssl, h:h + 1, :] for h in heads]
        inter = [jnp.dot(block_diag(blk.q[h], rowseg), c0[h].astype(BF16).reshape(nseg * M_DK, M_DK),
                         preferred_element_type=F32) for h in heads]
        qn = [jnp.sum(blk.q[h].astype(F32) * jnp.broadcast_to(n0[h], (nseg, seg, M_DK)).reshape(BLK, M_DK),
                      axis=-1, keepdims=True) for h in heads]
        gc = [r.g_s[rsl, h:h + 1] for h in heads]
        num = [gc[h] * inter[h] + blk.intra[h] for h in heads]
        den = [gc[h] * qn[h] + blk.rowsum[h] for h in heads]
        hh = [num[h] * (1.0 / jnp.maximum(jnp.abs(den[h]), r.em_s[rsl, h:h + 1])) for h in heads]
        for h in heads:
            r.hn_s[rsl, h * M_DK:(h + 1) * M_DK] = _rms_rows(hh[h])
        for h in heads:
            decay = g_blk[:, seg - 1:seg, h:h + 1]
            r.c[ssl, h] = decay * c0[h] + blk.upd[h].reshape(nseg, M_DK, M_DK)
            r.n[ssl, h:h + 1, :] = decay * n0[h] + blk.ksum[h]

    def conv_inputs():
        if lo == 0:
            r.ubuf_s[:, 6:8, :] = cv_in[...]
        u = proj(OFF_C, S_WIDTH) * proj(OFF_X, S_WIDTH)
        r.ubuf_s[:, 8 + lo:8 + lo + th, :] = u.reshape(nb, th, S_WIDTH)
        r.cv[...] = r.ubuf_s[:, lo + th + 6:lo + th + 8, :]

    def conv_outputs():
        bg = proj(OFF_B, S_WIDTH)
        cw = r.cw[...]
        yc = (cw[0:1] * r.ubuf_s[:, 6 + lo:6 + lo + th, :] + cw[1:2] * r.ubuf_s[:, 7 + lo:7 + lo + th, :]
              + cw[2:3] * r.ubuf_s[:, 8 + lo:8 + lo + th, :])
        ysv = (bg.reshape(nb, th, S_WIDTH) * yc).reshape(rows, S_WIDTH)
        for grp in range(S_GROUPS):
            cs = slice(M_WIDTH + grp * LANES, M_WIDTH + (grp + 1) * LANES)
            r.mix_s[fsl, cs] = (_rms_rows(ysv[:, grp * LANES:(grp + 1) * LANES]) * r.gmix[:, cs]).astype(BF16)

    def conv_half_out():
        v.y_conv = jnp.dot(r.mix_s[fsl, M_WIDTH:], r.wout[M_WIDTH:, :], preferred_element_type=F32)

    def finish():
        r.mix_s[fsl, :M_WIDTH] = (r.hn_s[fsl, :] * r.og_s[fsl, :] * r.gmix[:, :M_WIDTH]).astype(BF16)
        y = v.y_conv + jnp.dot(r.mix_s[fsl, :M_WIDTH], r.wout[:M_WIDTH, :], preferred_element_type=F32)
        r.xo[:, tsl, :] = v.x3 + r.gt[...] * y.reshape(nb, th, D_MODEL)

    blocks = [functools.partial(block_scores, blk0)]
    for bi in range(blk0, blk0 + nblk):
        if bi + 1 < blk0 + nblk:
            blocks.append(functools.partial(block_scores, bi + 1))
        blocks.append(functools.partial(block_state, bi))
    return types.SimpleNamespace(
        pre=[norm_gates, q_proj, gate_scalars, k_proj, v_proj], blocks=blocks,
        fillers=[o_proj, conv_inputs, conv_outputs, conv_half_out], finish=finish)


def _spread(chain, others):
    out, done = [], 0
    for i, step in enumerate(chain):
        out.append(step)
        want = ((i + 1) * len(others)) // len(chain)
        out.extend(others[done:want])
        done = want
    return out


def _mixer_steps(r, *, nb, tm, seg, first_tile, parts):
    th = tm // parts
    part = [_mixer_part(r, nb=nb, tm=tm, seg=seg, first_tile=first_tile, lo=i * th, th=th) for i in range(parts)]
    if parts == 1:
        a = part[0]
        return a.pre + _spread(a.blocks, a.fillers) + [a.finish]
    a, b = part
    return (a.pre
            + _spread(a.blocks, b.pre + a.fillers[:2])
            + _spread(b.blocks, a.fillers[2:] + [a.finish] + b.fillers[:2])
            + b.fillers[2:] + [b.finish])


_MIXER_REFS = (
    "x", "sh", "sc", "gt", "g1", "win", "bin", "cw", "gmix", "wout", "c0", "n0", "m0", "cv0",
    "xo", "c", "n", "m", "cv",
    "q_s", "k_s", "v_s", "og_s", "hn_s", "mix_s", "ubuf_s", "p_s", "g_s", "we_s", "em_s", "rt_s",
)


def _mixer_kernel(*refs, nb, tm, seg, nt, parts):
    r = types.SimpleNamespace(**dict(zip(_MIXER_REFS, refs, strict=True)))
    first_tile = None if nt == 1 else pl.program_id(1) == 0
    for step in _mixer_steps(r, nb=nb, tm=tm, seg=seg, first_tile=first_tile, parts=parts):
        step()


def _mixer(x, mod, mod_row0, layer, w, state, *, nb, tm, parts, name):
    bsz, t_len, _ = x.shape
    rows = nb * tm
    seg = min(BLK, tm // parts)
    assert (rows // parts) % BLK == 0 and bsz % nb == 0 and t_len % tm == 0 and mod_row0 % nb == 0
    assert tm & (tm - 1) == 0 and (nb == 1 or tm == t_len)
    c0, n0, m0, cv0, st_layer = state
    grid = (bsz // nb, t_len // tm)
    mrow = mod_row0 // nb

    def mod_spec(j):
        return pl.BlockSpec((None, None, nb, 1, D_MODEL), lambda b, t: (layer, j, mrow + b, 0, 0))

    def wspec(shape, resident=False):
        nd = len(shape)
        mode = {"pipeline_mode": pl.Buffered(1)} if resident else {}
        return pl.BlockSpec((None,) + shape, lambda b, t: (layer,) + (0,) * nd, **mode)

    def sspec(shape):
        nd = len(shape)
        return pl.BlockSpec((None, nb) + shape, lambda b, t: (st_layer, b) + (0,) * nd)

    def ospec(shape):
        nd = len(shape)
        return pl.BlockSpec((nb,) + shape, lambda b, t: (b,) + (0,) * nd)

    state_shapes = ((M_HEADS, M_DK, M_DK), (M_HEADS, M_DK), (1, LANES), (CONV_W - 1, S_WIDTH))
    in_specs = [
        pl.BlockSpec((nb, tm, D_MODEL), lambda b, t: (b, t, 0)),
        mod_spec(0), mod_spec(1), mod_spec(2),
        wspec((1, D_MODEL)),
        wspec((D_MODEL, N_Z), True),
        wspec((1, N_Z)),
        wspec((CONV_W, S_WIDTH)),
        wspec((1, D_MODEL)),
        wspec((D_MODEL, D_MODEL), True),
    ] + [sspec(s) for s in state_shapes]
    out_shape = (jax.ShapeDtypeStruct(x.shape, F32),) + tuple(
        jax.ShapeDtypeStruct((bsz,) + s, F32) for s in state_shapes)
    out_specs = (pl.BlockSpec((nb, tm, D_MODEL), lambda b, t: (b, t, 0)),) + tuple(ospec(s) for s in state_shapes)
    scratch = [
        pltpu.VMEM((rows, M_WIDTH), BF16),
        pltpu.VMEM((rows, M_WIDTH), BF16),
        pltpu.VMEM((rows, M_WIDTH), BF16),
        pltpu.VMEM((rows, M_WIDTH), F32),
        pltpu.VMEM((rows, M_WIDTH), F32),
        pltpu.VMEM((rows, D_MODEL), BF16),
        pltpu.VMEM((nb, tm + 8, S_WIDTH), F32),
        pltpu.VMEM((rows, LANES), F32),
        pltpu.VMEM((rows, LANES), F32),
        pltpu.VMEM((rows, LANES), F32),
        pltpu.VMEM((rows, LANES), F32),
        pltpu.VMEM((LANES, rows), F32),
    ]
    kern = functools.partial(_mixer_kernel, nb=nb, tm=tm, seg=seg, nt=t_len // tm, parts=parts)
    return pl.pallas_call(
        kern, out_shape=out_shape, grid=grid, in_specs=in_specs, out_specs=out_specs,
        scratch_shapes=scratch,
        compiler_params=pltpu.CompilerParams(dimension_semantics=("arbitrary", "arbitrary"),
                                             vmem_limit_bytes=VMEM_LIMIT),
        name=name,
    )(x, mod, mod, mod, w["g1"], w["w_in"], w["b_in"], w["conv_w"], w["g_mix"], w["w_out"],
      c0, n0, m0, cv0)


def _mlp_kernel(x_ref, sh_ref, sc_ref, gt_ref, g2_ref, wup_ref, wdn_ref, gf_ref, xo_ref, *, nb, tm, final):
    rows = nb * tm
    x3 = x_ref[...]
    h3 = _rms_rows(x3) * (g2_ref[...] * (1.0 + sc_ref[...])) + sh_ref[...]
    hb = h3.reshape(rows, D_MODEL).astype(BF16)
    acc = jnp.zeros((rows, D_MODEL), F32)
    for c in range(D_FF // FF_CHUNK):
        lo = c * FF_CHUNK
        up = jnp.dot(hb, wup_ref[:, lo:lo + FF_CHUNK], preferred_element_type=F32)
        act = jnp.square(jnp.maximum(up, 0.0)).astype(BF16)
        acc = acc + jnp.dot(act, wdn_ref[lo:lo + FF_CHUNK, :], preferred_element_type=F32)
    xn = x3 + gt_ref[...] * acc.reshape(nb, tm, D_MODEL)
    if final:
        xn = _rms_rows(xn) * gf_ref[...]
    xo_ref[...] = xn


def _mlp(x, mod, mod_row0, layer, w, g_final, *, nb, tm, final, name):
    bsz, t_len, _ = x.shape
    assert bsz % nb == 0 and t_len % tm == 0 and mod_row0 % nb == 0
    grid = (bsz // nb, t_len // tm)
    mrow = mod_row0 // nb

    def mod_spec(j):
        return pl.BlockSpec((None, None, nb, 1, D_MODEL), lambda b, t: (layer, j, mrow + b, 0, 0))

    in_specs = [
        pl.BlockSpec((nb, tm, D_MODEL), lambda b, t: (b, t, 0)),
        mod_spec(3), mod_spec(4), mod_spec(5),
        pl.BlockSpec((None, 1, D_MODEL), lambda b, t: (layer, 0, 0)),
        pl.BlockSpec((None, D_MODEL, D_FF), lambda b, t: (layer, 0, 0), pipeline_mode=pl.Buffered(1)),
        pl.BlockSpec((None, D_FF, D_MODEL), lambda b, t: (layer, 0, 0), pipeline_mode=pl.Buffered(1)),
        pl.BlockSpec((1, D_MODEL), lambda b, t: (0, 0)),
    ]
    kern = functools.partial(_mlp_kernel, nb=nb, tm=tm, final=final)
    return pl.pallas_call(
        kern, out_shape=jax.ShapeDtypeStruct(x.shape, F32), grid=grid, in_specs=in_specs,
        out_specs=pl.BlockSpec((nb, tm, D_MODEL), lambda b, t: (b, t, 0)),
        compiler_params=pltpu.CompilerParams(dimension_semantics=("arbitrary", "arbitrary"),
                                             vmem_limit_bytes=VMEM_LIMIT),
        name=name,
    )(x, mod, mod, mod, w["g2"], w["w_up"], w["w_down"], g_final)


def _mlp_stream_kernel(x_ref, sh_ref, sc_ref, gt_ref, g2_ref, wup_ref, wdn_ref, gf_ref, xo_ref, hb_s, acc_s,
                       *, nb, tm, final):
    rows = nb * tm
    c = pl.program_id(0)

    @pl.when(c == 0)
    def _():
        h3 = _rms_rows(x_ref[...]) * (g2_ref[...] * (1.0 + sc_ref[...])) + sh_ref[...]
        hb_s[...] = h3.reshape(rows, D_MODEL).astype(BF16)
        acc_s[...] = jnp.zeros((rows, D_MODEL), F32)

    up = jnp.dot(hb_s[...], wup_ref[...], preferred_element_type=F32)
    act = jnp.square(jnp.maximum(up, 0.0)).astype(BF16)
    acc_s[...] += jnp.dot(act, wdn_ref[...], preferred_element_type=F32)

    @pl.when(c == pl.num_programs(0) - 1)
    def _():
        xn = x_ref[...] + gt_ref[...] * acc_s[...].reshape(nb, tm, D_MODEL)
        if final:
            xn = _rms_rows(xn) * gf_ref[...]
        xo_ref[...] = xn


def _mlp_stream(x, mod, layer, w, g_final, *, final, name):
    nb, tm, _ = x.shape

    def mod_spec(j):
        return pl.BlockSpec((None, None, nb, 1, D_MODEL), lambda c: (layer, j, 0, 0, 0))

    in_specs = [
        pl.BlockSpec((nb, tm, D_MODEL), lambda c: (0, 0, 0)),
        mod_spec(3), mod_spec(4), mod_spec(5),
        pl.BlockSpec((None, 1, D_MODEL), lambda c: (layer, 0, 0)),
        pl.BlockSpec((None, D_MODEL, FF_CHUNK), lambda c: (layer, 0, c)),
        pl.BlockSpec((None, FF_CHUNK, D_MODEL), lambda c: (layer, c, 0)),
        pl.BlockSpec((1, D_MODEL), lambda c: (0, 0)),
    ]
    return pl.pallas_call(
        functools.partial(_mlp_stream_kernel, nb=nb, tm=tm, final=final),
        out_shape=jax.ShapeDtypeStruct(x.shape, F32), grid=(D_FF // FF_CHUNK,), in_specs=in_specs,
        out_specs=pl.BlockSpec((nb, tm, D_MODEL), lambda c: (0, 0, 0)),
        scratch_shapes=[pltpu.VMEM((nb * tm, D_MODEL), BF16), pltpu.VMEM((nb * tm, D_MODEL), F32)],
        compiler_params=pltpu.CompilerParams(dimension_semantics=("arbitrary",), vmem_limit_bytes=VMEM_LIMIT),
        name=name,
    )(x, mod, mod, mod, w["g2"], w["w_up"], w["w_down"], g_final)


def kernel(x_prompt, x_sample, c_prompt, c_sample, state_C, state_n, state_m, state_conv, w_ada, b_ada,
           g_norm1, w_in, b_in, conv_w, g_mix_out, w_out, g_norm2, w_up, w_down, g_final):
    bp, sp, _ = x_prompt.shape
    bs, ss, _ = x_sample.shape

    def relayout_bias(a):
        gi = a[..., REF_OFF_I:REF_OFF_F]
        gf = a[..., REF_OFF_F:REF_OFF_B]
        pad = [(0, 0)] * (a.ndim - 1) + [(0, LANES - M_HEADS)]
        return jnp.concatenate([a[..., :REF_OFF_I], a[..., REF_OFF_B:], jnp.pad(gi, pad), jnp.pad(gf, pad)], axis=-1)

    w = {
        "g1": g_norm1.reshape(DEPTH, 1, D_MODEL),
        "w_in": _relayout_w_in(w_in),
        "b_in": relayout_bias(b_in).reshape(DEPTH, 1, N_Z),
        "conv_w": conv_w,
        "g_mix": g_mix_out.reshape(DEPTH, 1, D_MODEL),
        "w_out": w_out,
        "g2": g_norm2.reshape(DEPTH, 1, D_MODEL),
        "w_up": w_up,
        "w_down": w_down,
    }
    gfin = g_final.reshape(1, D_MODEL)

    mod = _modulation(jnp.concatenate([c_sample, c_prompt], axis=0), w_ada, b_ada)

    zero_state = (jnp.zeros((1, bp, M_HEADS, M_DK, M_DK), F32), jnp.zeros((1, bp, M_HEADS, M_DK), F32),
                  jnp.zeros((1, bp, 1, LANES), F32), jnp.zeros((1, bp, CONV_W - 1, S_WIDTH), F32))
    m_pad = jnp.pad(state_m, ((0, 0), (0, 0), (0, LANES - M_HEADS))).reshape(DEPTH, bs, 1, LANES)

    tm_p = 1024
    nb_s = 16
    xp, xs = x_prompt, x_sample
    outs = [[] for _ in range(8)]
    for l in range(DEPTH):
        last = l == DEPTH - 1
        xp, c1, n1, m1, cv1 = _mixer(xp, mod, bs, l, w, zero_state + (0,), nb=1, tm=tm_p, parts=2,
                                     name=f"mixer_prompt_{l}")
        xs, c2, n2, m2, cv2 = _mixer(xs, mod, 0, l, w, (state_C, state_n, m_pad, state_conv, l),
                                     nb=nb_s, tm=ss, parts=1, name=f"mixer_sample_{l}")
        for lst, val in zip(outs, (c1, n1, m1[:, 0, :M_HEADS], cv1, c2, n2, m2[:, 0, :M_HEADS], cv2)):
            lst.append(val)
        xp = _mlp(xp, mod, bs, l, w, gfin, nb=1, tm=tm_p, final=last, name=f"mlp_prompt_{l}")
        xs = _mlp_stream(xs, mod, l, w, gfin, final=last, name=f"mlp_sample_{l}")
    return (xp, xs) + tuple(jnp.stack(o) for o in outs)
```

```python
import functools
import types

import jax
import jax.numpy as jnp
from jax import lax
from jax.experimental import pallas as pl
from jax.experimental.pallas import tpu as pltpu

F32 = jnp.float32
BF16 = jnp.bfloat16

D_MODEL = 1024
DEPTH = 4
M_HEADS = 4
M_DK = 128
M_WIDTH = M_HEADS * M_DK
S_WIDTH = 512
S_GROUPS = 4
CONV_W = 3
D_FF = 4 * D_MODEL
N_MOD = 6
EPS = 1e-6

REF_OFF_I = 4 * M_WIDTH
REF_OFF_F = REF_OFF_I + M_HEADS
REF_OFF_B = REF_OFF_F + M_HEADS
REF_N_IN = REF_OFF_B + 3 * S_WIDTH

LANES = 128
BLK = 128
OFF_Q, OFF_K, OFF_V, OFF_O = 0, 512, 1024, 1536
OFF_B, OFF_C, OFF_X = 2048, 2560, 3072
OFF_GI, OFF_GF = 3584, 3712
N_Z = 3840
FF_CHUNK = 1024
RELAYOUT_COLS = 256
VMEM_LIMIT = 58 * 1024 * 1024


def _rms_rows(x):
    return x * lax.rsqrt(jnp.mean(x * x, axis=-1, keepdims=True) + EPS)


def _log_sigmoid(x):
    return jnp.minimum(x, 0.0) - jnp.log1p(jnp.exp(-jnp.abs(x)))


MOD_PER_STEP = 2


def _mod_kernel(c_ref, w_ref, b_ref, o_ref):
    c = c_ref[...]
    a = (c * jax.nn.sigmoid(c)).astype(BF16)
    res = jnp.dot(a, w_ref[...].astype(BF16), preferred_element_type=F32)
    for k in range(MOD_PER_STEP):
        o_ref[k] = res[:, k * D_MODEL:(k + 1) * D_MODEL] + b_ref[k]


def _modulation(c_all, w_ada, b_ada):
    nrow = c_all.shape[0]
    b4 = b_ada.reshape(DEPTH, N_MOD, 1, D_MODEL)
    out = pl.pallas_call(
        _mod_kernel,
        out_shape=jax.ShapeDtypeStruct((DEPTH, N_MOD, nrow, D_MODEL), F32),
        grid=(DEPTH, N_MOD // MOD_PER_STEP),
        in_specs=[
            pl.BlockSpec((nrow, D_MODEL), lambda l, j: (0, 0)),
            pl.BlockSpec((None, D_MODEL, MOD_PER_STEP * D_MODEL), lambda l, j: (l, 0, j)),
            pl.BlockSpec((None, MOD_PER_STEP, 1, D_MODEL), lambda l, j: (l, j, 0, 0)),
        ],
        out_specs=pl.BlockSpec((None, MOD_PER_STEP, nrow, D_MODEL), lambda l, j: (l, j, 0, 0)),
        compiler_params=pltpu.CompilerParams(dimension_semantics=("arbitrary", "arbitrary"),
                                             vmem_limit_bytes=VMEM_LIMIT),
        name="adaln_modulation",
    )(c_all, w_ada, b4)
    return out.reshape(DEPTH, N_MOD, nrow, 1, D_MODEL)


def _relayout_kernel(wt_ref, o_ref):
    def put(col0, row0, ncol):
        for c in range(0, ncol, RELAYOUT_COLS):
            o_ref[:, col0 + c:col0 + c + RELAYOUT_COLS] = (
                wt_ref[row0 + c:row0 + c + RELAYOUT_COLS, :].T.astype(BF16))

    put(0, 0, REF_OFF_I)
    put(OFF_B, REF_OFF_B, REF_N_IN - REF_OFF_B)
    g = wt_ref[REF_OFF_I:REF_OFF_I + LANES, :].T
    lane = lax.broadcasted_iota(jnp.int32, g.shape, 1)
    o_ref[:, OFF_GI:OFF_GF] = jnp.where(lane < M_HEADS, g, 0.0).astype(BF16)
    o_ref[:, OFF_GF:N_Z] = jnp.where(lane < M_HEADS, pltpu.roll(g, LANES - M_HEADS, axis=1), 0.0).astype(BF16)


def _relayout_w_in(w_in):
    wt = jnp.swapaxes(w_in, 1, 2)
    return pl.pallas_call(
        _relayout_kernel,
        out_shape=jax.ShapeDtypeStruct((DEPTH, D_MODEL, N_Z), BF16),
        grid=(DEPTH,),
        in_specs=[pl.BlockSpec((None, REF_N_IN, D_MODEL), lambda l: (l, 0, 0))],
        out_specs=pl.BlockSpec((None, D_MODEL, N_Z), lambda l: (l, 0, 0)),
        compiler_params=pltpu.CompilerParams(dimension_semantics=("arbitrary",), vmem_limit_bytes=VMEM_LIMIT),
        name="relayout_w_in",
    )(wt)


def _mixer_part(r, *, nb, tm, seg, first_tile, lo, th):
    assert th == tm or nb == 1
    rows = nb * th
    fsl = slice(lo, lo + rows)
    tsl = slice(lo, lo + th)
    nblk = rows // BLK
    blk0 = lo // BLK
    nseg = BLK // seg
    v = types.SimpleNamespace(blk={})

    if first_tile is None:
        assert lo == 0
        c_in, n_in, m_in, cv_in = r.c0, r.n0, r.m0, r.cv0
    else:
        c_in, n_in, m_in, cv_in = r.c, r.n, r.m, r.cv

    def proj(off, width):
        return (jnp.dot(v.hb, r.win[:, off:off + width], preferred_element_type=F32)
                + r.bin[:, off:off + width])

    def norm_gates():
        if first_tile is not None and lo == 0:
            @pl.when(first_tile)
            def _():
                r.c[...] = r.c0[...]
                r.n[...] = r.n0[...]
                r.m[...] = r.m0[...]
                r.cv[...] = r.cv0[...]

        v.x3 = r.x[:, tsl, :]
        h3 = _rms_rows(v.x3) * (r.g1[...] * (1.0 + r.sc[...])) + r.sh[...]
        v.hb = h3.reshape(rows, D_MODEL).astype(BF16)
        v.gates = proj(OFF_GI, 2 * LANES)

    def q_proj():
        r.q_s[fsl, :] = proj(OFF_Q, M_WIDTH).astype(BF16)

    def gate_scalars():
        logi = v.gates[:, :LANES]
        logf = _log_sigmoid(v.gates[:, LANES:])
        rin = lax.broadcasted_iota(jnp.int32, (rows, LANES), 0) & (th - 1)
        bt = logf
        shift = 1
        while shift < th:
            bt = bt + jnp.where(rin >= shift, pltpu.roll(bt, shift, axis=0), 0.0)
            shift *= 2
        rr = logi - bt
        cm = rr
        shift = 1
        while shift < th:
            cm = jnp.maximum(cm, jnp.where(rin >= shift, pltpu.roll(cm, shift, axis=0), -jnp.inf))
            shift *= 2
        m0b = jnp.broadcast_to(m_in[...], (nb, th, LANES)).reshape(rows, LANES)
        cmx = jnp.maximum(m0b, cm)
        p = -cmx
        m = bt + cmx
        r.m[...] = m.reshape(nb, th, LANES)[:, th - 1:th, :]
        nchunk = rows // seg
        p3 = p.reshape(nchunk, seg, LANES)
        p_before = jnp.where(rin == 0, -m0b, pltpu.roll(p, 1, axis=0)).reshape(nchunk, seg, LANES)[:, 0:1, :]
        r.p_s[fsl, :] = p
        r.g_s[fsl, :] = jnp.exp(p3 - p_before).reshape(rows, LANES)
        r.we_s[fsl, :] = jnp.exp(p3[:, seg - 1:seg, :] + rr.reshape(nchunk, seg, LANES)).reshape(rows, LANES)
        r.em_s[fsl, :] = jnp.exp(-m)
        for bi in range(nblk):
            r.rt_s[:, lo + bi * BLK:lo + (bi + 1) * BLK] = rr[bi * BLK:(bi + 1) * BLK, :].T

    def k_proj():
        r.k_s[fsl, :] = (proj(OFF_K, M_WIDTH) * (M_DK ** -0.5)).astype(BF16)

    def v_proj():
        r.v_s[fsl, :] = proj(OFF_V, M_WIDTH).astype(BF16)

    def o_proj():
        r.og_s[fsl, :] = jax.nn.sigmoid(proj(OFF_O, M_WIDTH))

    def block_diag(a, rowseg):
        if nseg == 1:
            return a
        zero = jnp.zeros_like(a)
        return jnp.concatenate([jnp.where(rowseg == j, a, zero) for j in range(nseg)], axis=1)

    def block_consts(bi):
        ri = lax.broadcasted_iota(jnp.int32, (BLK, BLK), 0)
        ci = lax.broadcasted_iota(jnp.int32, (BLK, BLK), 1)
        sh = seg.bit_length() - 1
        rowseg = ri >> sh
        mask = ci <= ri
        if nseg > 1:
            mask = mask & (rowseg == (ci >> sh))
        r0 = bi * BLK
        rsl = slice(r0, r0 + BLK)
        seq0 = r0 // tm
        return rowseg, mask, rsl, slice(seq0, seq0 + nseg)

    def block_scores(bi):
        rowseg, mask, rsl, _ = block_consts(bi)
        hsl = [slice(h * M_DK, (h + 1) * M_DK) for h in range(M_HEADS)]
        q = [r.q_s[rsl, hs] for hs in hsl]
        k = [r.k_s[rsl, hs] for hs in hsl]
        vv = [r.v_s[rsl, hs] for hs in hsl]
        s = [lax.dot_general(q[h], k[h], (((1,), (1,)), ((), ())), preferred_element_type=F32)
             for h in range(M_HEADS)]
        sw = [s[h] * jnp.exp(jnp.where(mask, r.p_s[rsl, h:h + 1] + r.rt_s[h:h + 1, rsl], -jnp.inf))
              for h in range(M_HEADS)]
        kw = [k[h].astype(F32) * r.we_s[rsl, h:h + 1] for h in range(M_HEADS)]
        blk = types.SimpleNamespace(q=q)
        blk.rowsum = [jnp.sum(sw[h], axis=-1, keepdims=True) for h in range(M_HEADS)]
        blk.intra = [jnp.dot(sw[h].astype(BF16), vv[h], preferred_element_type=F32) for h in range(M_HEADS)]
        blk.upd = [lax.dot_general(block_diag(kw[h].astype(BF16), rowseg), vv[h], (((0,), (0,)), ((), ())),
                                   preferred_element_type=F32) for h in range(M_HEADS)]
        blk.ksum = [jnp.sum(kw[h].reshape(nseg, seg, M_DK), axis=1, keepdims=True) for h in range(M_HEADS)]
        v.blk[bi] = blk

    def block_state(bi):
        rowseg, _, rsl, ssl = block_consts(bi)
        blk = v.blk.pop(bi)
        g_blk = r.g_s[rsl, :].reshape(nseg, seg, LANES)
        heads = range(M_HEADS)
        c0 = [c_in[ssl, h] for h in heads]
        n0 = [n_in[ssl, h:h + 1, :] for h in heads]
        inter = [jnp.dot(block_diag(blk.q[h], rowseg), c0[h].astype(BF16).reshape(nseg * M_DK, M_DK),
                         preferred_element_type=F32) for h in heads]
        qn = [jnp.sum(blk.q[h].astype(F32) * jnp.broadcast_to(n0[h], (nseg, seg, M_DK)).reshape(BLK, M_DK),
                      axis=-1, keepdims=True) for h in heads]
        gc = [r.g_s[rsl, h:h + 1] for h in heads]
        num = [gc[h] * inter[h] + blk.intra[h] for h in heads]
        den = [gc[h] * qn[h] + blk.rowsum[h] for h in heads]
        hh = [num[h] * (1.0 / jnp.maximum(jnp.abs(den[h]), r.em_s[rsl, h:h + 1])) for h in heads]
        for h in heads:
            r.hn_s[rsl, h * M_DK:(h + 1) * M_DK] = _rms_rows(hh[h])
        for h in heads:
            decay = g_blk[:, seg - 1:seg, h:h + 1]
            r.c[ssl, h] = decay * c0[h] + blk.upd[h].reshape(nseg, M_DK, M_DK)
            r.n[ssl, h:h + 1, :] = decay * n0[h] + blk.ksum[h]

    def conv_inputs():
        if lo == 0:
            r.ubuf_s[:, 6:8, :] = cv_in[...]
        u = proj(OFF_C, S_WIDTH) * proj(OFF_X, S_WIDTH)
        r.ubuf_s[:, 8 + lo:8 + lo + th, :] = u.reshape(nb, th, S_WIDTH)
        r.cv[...] = r.ubuf_s[:, lo + th + 6:lo + th + 8, :]

    def conv_outputs():
        bg = proj(OFF_B, S_WIDTH)
        cw = r.cw[...]
        yc = (cw[0:1] * r.ubuf_s[:, 6 + lo:6 + lo + th, :] + cw[1:2] * r.ubuf_s[:, 7 + lo:7 + lo + th, :]
              + cw[2:3] * r.ubuf_s[:, 8 + lo:8 + lo + th, :])
        ysv = (bg.reshape(nb, th, S_WIDTH) * yc).reshape(rows, S_WIDTH)
        for grp in range(S_GROUPS):
            cs = slice(M_WIDTH + grp * LANES, M_WIDTH + (grp + 1) * LANES)
            r.mix_s[fsl, cs] = (_rms_rows(ysv[:, grp * LANES:(grp + 1) * LANES]) * r.gmix[:, cs]).astype(BF16)

    def conv_half_out():
        v.y_conv = jnp.dot(r.mix_s[fsl, M_WIDTH:], r.wout[M_WIDTH:, :], preferred_element_type=F32)

    def finish():
        r.mix_s[fsl, :M_WIDTH] = (r.hn_s[fsl, :] * r.og_s[fsl, :] * r.gmix[:, :M_WIDTH]).astype(BF16)
        y = v.y_conv + jnp.dot(r.mix_s[fsl, :M_WIDTH], r.wout[:M_WIDTH, :], preferred_element_type=F32)
        r.xo[:, tsl, :] = v.x3 + r.gt[...] * y.reshape(nb, th, D_MODEL)

    blocks = [functools.partial(block_scores, blk0)]
    for bi in range(blk0, blk0 + nblk):
        if bi + 1 < blk0 + nblk:
            blocks.append(functools.partial(block_scores, bi + 1))
        blocks.append(functools.partial(block_state, bi))
    return types.SimpleNamespace(
        pre=[norm_gates, q_proj, gate_scalars, k_proj, v_proj], blocks=blocks,
        fillers=[o_proj, conv_inputs, conv_outputs, conv_half_out], finish=finish)


def _spread(chain, others):
    out, done = [], 0
    for i, step in enumerate(chain):
        out.append(step)
        want = ((i + 1) * len(others)) // len(chain)
        out.extend(others[done:want])
        done = want
    return out


def _mixer_steps(r, *, nb, tm, seg, first_tile, parts):
    th = tm // parts
    part = [_mixer_part(r, nb=nb, tm=tm, seg=seg, first_tile=first_tile, lo=i * th, th=th) for i in range(parts)]
    if parts == 1:
        a = part[0]
        return a.pre + _spread(a.blocks, a.fillers) + [a.finish]
    a, b = part
    return (a.pre
            + _spread(a.blocks, b.pre + a.fillers[:2])
            + _spread(b.blocks, a.fillers[2:] + [a.finish] + b.fillers[:2])
            + b.fillers[2:] + [b.finish])


_MIXER_REFS = (
    "x", "sh", "sc", "gt", "g1", "win", "bin", "cw", "gmix", "wout", "c0", "n0", "m0", "cv0",
    "xo", "c", "n", "m", "cv",
    "q_s", "k_s", "v_s", "og_s", "hn_s", "mix_s", "ubuf_s", "p_s", "g_s", "we_s", "em_s", "rt_s",
)


def _mixer_kernel(*refs, nb, tm, seg, nt, parts, n_cast):
    n_in, n_out = _MIXER_REFS.index("xo"), _MIXER_REFS.index("q_s")
    cast_in = refs[n_in:n_in + n_cast]
    cast_out = refs[n_out + n_cast:n_out + 2 * n_cast]
    refs = refs[:n_in] + refs[n_in + n_cast:n_out + n_cast] + refs[n_out + 2 * n_cast:]
    r = types.SimpleNamespace(**dict(zip(_MIXER_REFS, refs, strict=True)))
    first_tile = None if nt == 1 else pl.program_id(1) == 0
    for step in _mixer_steps(r, nb=nb, tm=tm, seg=seg, first_tile=first_tile, parts=parts):
        step()
    for src_ref, dst_ref in zip(cast_in, cast_out):
        dst_ref[...] = src_ref[...].astype(BF16)


def _mixer(x, mod, mod_row0, layer, w, state, *, nb, tm, parts, name, cast=()):
    bsz, t_len, _ = x.shape
    rows = nb * tm
    seg = min(BLK, tm // parts)
    assert (rows // parts) % BLK == 0 and bsz % nb == 0 and t_len % tm == 0 and mod_row0 % nb == 0
    assert tm & (tm - 1) == 0 and (nb == 1 or tm == t_len)
    c0, n0, m0, cv0, st_layer = state
    grid = (bsz // nb, t_len // tm)
    mrow = mod_row0 // nb

    def mod_spec(j):
        return pl.BlockSpec((None, None, nb, 1, D_MODEL), lambda b, t: (layer, j, mrow + b, 0, 0))

    def wspec(shape, resident=False):
        nd = len(shape)
        mode = {"pipeline_mode": pl.Buffered(1)} if resident else {}
        return pl.BlockSpec((None,) + shape, lambda b, t: (layer,) + (0,) * nd, **mode)

    def sspec(shape):
        nd = len(shape)
        return pl.BlockSpec((None, nb) + shape, lambda b, t: (st_layer, b) + (0,) * nd)

    def ospec(shape):
        nd = len(shape)
        return pl.BlockSpec((nb,) + shape, lambda b, t: (b,) + (0,) * nd)

    state_shapes = ((M_HEADS, M_DK, M_DK), (M_HEADS, M_DK), (1, LANES), (CONV_W - 1, S_WIDTH))
    in_specs = [
        pl.BlockSpec((nb, tm, D_MODEL), lambda b, t: (b, t, 0)),
        mod_spec(0), mod_spec(1), mod_spec(2),
        wspec((1, D_MODEL)),
        wspec((D_MODEL, N_Z), True),
        wspec((1, N_Z)),
        wspec((CONV_W, S_WIDTH)),
        wspec((1, D_MODEL)),
        wspec((D_MODEL, D_MODEL), True),
    ] + [sspec(s) for s in state_shapes]
    out_shape = (jax.ShapeDtypeStruct(x.shape, F32),) + tuple(
        jax.ShapeDtypeStruct((bsz,) + s, F32) for s in state_shapes)
    out_specs = (pl.BlockSpec((nb, tm, D_MODEL), lambda b, t: (b, t, 0)),) + tuple(ospec(s) for s in state_shapes)
    nstep = grid[0] * grid[1]
    for a in cast:
        slab = a.shape[1] // nstep
        assert a.shape[1] % nstep == 0 and slab % 16 == 0
        in_specs.append(pl.BlockSpec((None, slab, a.shape[2]), lambda b, t: (layer, b * grid[1] + t, 0)))
        out_shape += (jax.ShapeDtypeStruct(a.shape[1:], BF16),)
        out_specs += (pl.BlockSpec((slab, a.shape[2]), lambda b, t: (b * grid[1] + t, 0)),)
    scratch = [
        pltpu.VMEM((rows, M_WIDTH), BF16),
        pltpu.VMEM((rows, M_WIDTH), BF16),
        pltpu.VMEM((rows, M_WIDTH), BF16),
        pltpu.VMEM((rows, M_WIDTH), F32),
        pltpu.VMEM((rows, M_WIDTH), F32),
        pltpu.VMEM((rows, D_MODEL), BF16),
        pltpu.VMEM((nb, tm + 8, S_WIDTH), F32),
        pltpu.VMEM((rows, LANES), F32),
        pltpu.VMEM((rows, LANES), F32),
        pltpu.VMEM((rows, LANES), F32),
        pltpu.VMEM((rows, LANES), F32),
        pltpu.VMEM((LANES, rows), F32),
    ]
    kern = functools.partial(_mixer_kernel, nb=nb, tm=tm, seg=seg, nt=t_len // tm, parts=parts, n_cast=len(cast))
    return pl.pallas_call(
        kern, out_shape=out_shape, grid=grid, in_specs=in_specs, out_specs=out_specs,
        scratch_shapes=scratch,
        compiler_params=pltpu.CompilerParams(dimension_semantics=("arbitrary", "arbitrary"),
                                             vmem_limit_bytes=VMEM_LIMIT),
        name=name,
    )(x, mod, mod, mod, w["g1"], w["w_in"], w["b_in"], w["conv_w"], w["g_mix"], w["w_out"],
      c0, n0, m0, cv0, *cast)


def _mlp_kernel(x_ref, sh_ref, sc_ref, gt_ref, g2_ref, wup_ref, wdn_ref, gf_ref, xo_ref, *, nb, tm, final):
    rows = nb * tm
    x3 = x_ref[...]
    h3 = _rms_rows(x3) * (g2_ref[...] * (1.0 + sc_ref[...])) + sh_ref[...]
    hb = h3.reshape(rows, D_MODEL).astype(BF16)
    acc = jnp.zeros((rows, D_MODEL), F32)
    for c in range(D_FF // FF_CHUNK):
        lo = c * FF_CHUNK
        up = jnp.dot(hb, wup_ref[:, lo:lo + FF_CHUNK], preferred_element_type=F32)
        act = jnp.square(jnp.maximum(up, 0.0)).astype(BF16)
        acc = acc + jnp.dot(act, wdn_ref[lo:lo + FF_CHUNK, :], preferred_element_type=F32)
    xn = x3 + gt_ref[...] * acc.reshape(nb, tm, D_MODEL)
    if final:
        xn = _rms_rows(xn) * gf_ref[...]
    xo_ref[...] = xn


def _mlp(x, mod, mod_row0, layer, w, w_up, w_down, g_final, *, nb, tm, final, name):
    bsz, t_len, _ = x.shape
    assert bsz % nb == 0 and t_len % tm == 0 and mod_row0 % nb == 0
    grid = (bsz // nb, t_len // tm)
    mrow = mod_row0 // nb

    def mod_spec(j):
        return pl.BlockSpec((None, None, nb, 1, D_MODEL), lambda b, t: (layer, j, mrow + b, 0, 0))

    in_specs = [
        pl.BlockSpec((nb, tm, D_MODEL), lambda b, t: (b, t, 0)),
        mod_spec(3), mod_spec(4), mod_spec(5),
        pl.BlockSpec((None, 1, D_MODEL), lambda b, t: (layer, 0, 0)),
        pl.BlockSpec((D_MODEL, D_FF), lambda b, t: (0, 0), pipeline_mode=pl.Buffered(1)),
        pl.BlockSpec((D_FF, D_MODEL), lambda b, t: (0, 0), pipeline_mode=pl.Buffered(1)),
        pl.BlockSpec((1, D_MODEL), lambda b, t: (0, 0)),
    ]
    kern = functools.partial(_mlp_kernel, nb=nb, tm=tm, final=final)
    return pl.pallas_call(
        kern, out_shape=jax.ShapeDtypeStruct(x.shape, F32), grid=grid, in_specs=in_specs,
        out_specs=pl.BlockSpec((nb, tm, D_MODEL), lambda b, t: (b, t, 0)),
        compiler_params=pltpu.CompilerParams(dimension_semantics=("arbitrary", "arbitrary"),
                                             vmem_limit_bytes=VMEM_LIMIT),
        name=name,
    )(x, mod, mod, mod, w["g2"], w_up, w_down, g_final)


def _mlp_stream_kernel(x_ref, sh_ref, sc_ref, gt_ref, g2_ref, wup_ref, wdn_ref, gf_ref, xo_ref, hb_s, acc_s,
                       *, nb, tm, final):
    rows = nb * tm
    c = pl.program_id(0)

    @pl.when(c == 0)
    def _():
        h3 = _rms_rows(x_ref[...]) * (g2_ref[...] * (1.0 + sc_ref[...])) + sh_ref[...]
        hb_s[...] = h3.reshape(rows, D_MODEL).astype(BF16)
        acc_s[...] = jnp.zeros((rows, D_MODEL), F32)

    up = jnp.dot(hb_s[...], wup_ref[...], preferred_element_type=F32)
    act = jnp.square(jnp.maximum(up, 0.0)).astype(BF16)
    acc_s[...] += jnp.dot(act, wdn_ref[...], preferred_element_type=F32)

    @pl.when(c == pl.num_programs(0) - 1)
    def _():
        xn = x_ref[...] + gt_ref[...] * acc_s[...].reshape(nb, tm, D_MODEL)
        if final:
            xn = _rms_rows(xn) * gf_ref[...]
        xo_ref[...] = xn


def _mlp_stream(x, mod, layer, w, w_up, w_down, g_final, *, final, name):
    nb, tm, _ = x.shape

    def mod_spec(j):
        return pl.BlockSpec((None, None, nb, 1, D_MODEL), lambda c: (layer, j, 0, 0, 0))

    in_specs = [
        pl.BlockSpec((nb, tm, D_MODEL), lambda c: (0, 0, 0)),
        mod_spec(3), mod_spec(4), mod_spec(5),
        pl.BlockSpec((None, 1, D_MODEL), lambda c: (layer, 0, 0)),
        pl.BlockSpec((D_MODEL, FF_CHUNK), lambda c: (0, c)),
        pl.BlockSpec((FF_CHUNK, D_MODEL), lambda c: (c, 0)),
        pl.BlockSpec((1, D_MODEL), lambda c: (0, 0)),
    ]
    return pl.pallas_call(
        functools.partial(_mlp_stream_kernel, nb=nb, tm=tm, final=final),
        out_shape=jax.ShapeDtypeStruct(x.shape, F32), grid=(D_FF // FF_CHUNK,), in_specs=in_specs,
        out_specs=pl.BlockSpec((nb, tm, D_MODEL), lambda c: (0, 0, 0)),
        scratch_shapes=[pltpu.VMEM((nb * tm, D_MODEL), BF16), pltpu.VMEM((nb * tm, D_MODEL), F32)],
        compiler_params=pltpu.CompilerParams(dimension_semantics=("arbitrary",), vmem_limit_bytes=VMEM_LIMIT),
        name=name,
    )(x, mod, mod, mod, w["g2"], w_up, w_down, g_final)


def kernel(x_prompt, x_sample, c_prompt, c_sample, state_C, state_n, state_m, state_conv, w_ada, b_ada,
           g_norm1, w_in, b_in, conv_w, g_mix_out, w_out, g_norm2, w_up, w_down, g_final):
    bp, sp, _ = x_prompt.shape
    bs, ss, _ = x_sample.shape

    def relayout_bias(a):
        gi = a[..., REF_OFF_I:REF_OFF_F]
        gf = a[..., REF_OFF_F:REF_OFF_B]
        pad = [(0, 0)] * (a.ndim - 1) + [(0, LANES - M_HEADS)]
        return jnp.concatenate([a[..., :REF_OFF_I], a[..., REF_OFF_B:], jnp.pad(gi, pad), jnp.pad(gf, pad)], axis=-1)

    w = {
        "g1": g_norm1.reshape(DEPTH, 1, D_MODEL),
        "w_in": _relayout_w_in(w_in),
        "b_in": relayout_bias(b_in).reshape(DEPTH, 1, N_Z),
        "conv_w": conv_w,
        "g_mix": g_mix_out.reshape(DEPTH, 1, D_MODEL),
        "w_out": w_out,
        "g2": g_norm2.reshape(DEPTH, 1, D_MODEL),
    }
    gfin = g_final.reshape(1, D_MODEL)

    mod = _modulation(jnp.concatenate([c_sample, c_prompt], axis=0), w_ada, b_ada)

    zero_state = (jnp.zeros((1, bp, M_HEADS, M_DK, M_DK), F32), jnp.zeros((1, bp, M_HEADS, M_DK), F32),
                  jnp.zeros((1, bp, 1, LANES), F32), jnp.zeros((1, bp, CONV_W - 1, S_WIDTH), F32))
    m_pad = jnp.pad(state_m, ((0, 0), (0, 0), (0, LANES - M_HEADS))).reshape(DEPTH, bs, 1, LANES)

    tm_p = 1024
    nb_s = 16
    xp, xs = x_prompt, x_sample
    outs = [[] for _ in range(8)]
    for l in range(DEPTH):
        last = l == DEPTH - 1
        xp, c1, n1, m1, cv1, wu, wd = _mixer(xp, mod, bs, l, w, zero_state + (0,), nb=1, tm=tm_p, parts=2,
                                             name=f"mixer_prompt_{l}", cast=(w_up, w_down))
        xs, c2, n2, m2, cv2 = _mixer(xs, mod, 0, l, w, (state_C, state_n, m_pad, state_conv, l),
                                     nb=nb_s, tm=ss, parts=1, name=f"mixer_sample_{l}")
        for lst, val in zip(outs, (c1, n1, m1[:, 0, :M_HEADS], cv1, c2, n2, m2[:, 0, :M_HEADS], cv2)):
            lst.append(val)
        xp = _mlp(xp, mod, bs, l, w, wu, wd, gfin, nb=1, tm=tm_p, final=last, name=f"mlp_prompt_{l}")
        xs = _mlp_stream(xs, mod, l, w, wu, wd, gfin, final=last, name=f"mlp_sample_{l}")
    return (xp, xs) + tuple(jnp.stack(o) for o in outs)
```

```python
import functools
import types

import jax
import jax.numpy as jnp
from jax import lax
from jax.experimental import pallas as pl
from jax.experimental.pallas import tpu as pltpu

F32 = jnp.float32
BF16 = jnp.bfloat16

D_MODEL = 1024
DEPTH = 4
M_HEADS = 4
M_DK = 128
M_WIDTH = M_HEADS * M_DK
S_WIDTH = 512
S_GROUPS = 4
CONV_W = 3
D_FF = 4 * D_MODEL
N_MOD = 6
EPS = 1e-6

REF_OFF_I = 4 * M_WIDTH
REF_OFF_F = REF_OFF_I + M_HEADS
REF_OFF_B = REF_OFF_F + M_HEADS
REF_N_IN = REF_OFF_B + 3 * S_WIDTH

LANES = 128
BLK = 128
OFF_Q, OFF_K, OFF_V, OFF_O = 0, 512, 1024, 1536
OFF_B, OFF_C, OFF_X = 2048, 2560, 3072
OFF_GI, OFF_GF = 3584, 3712
N_Z = 3840
FF_CHUNK = 1024
RELAYOUT_COLS = 256
VMEM_LIMIT = 58 * 1024 * 1024


def _rms_rows(x):
    return x * lax.rsqrt(jnp.mean(x * x, axis=-1, keepdims=True) + EPS)


def _log_sigmoid(x):
    return jnp.minimum(x, 0.0) - jnp.log1p(jnp.exp(-jnp.abs(x)))


MOD_PER_STEP = 2


def _mod_kernel(c_ref, w_ref, b_ref, o_ref):
    c = c_ref[...]
    a = (c * jax.nn.sigmoid(c)).astype(BF16)
    res = jnp.dot(a, w_ref[...].astype(BF16), preferred_element_type=F32)
    for k in range(MOD_PER_STEP):
        o_ref[k] = res[:, k * D_MODEL:(k + 1) * D_MODEL] + b_ref[k]


def _modulation(c_all, w_ada, b_ada):
    nrow = c_all.shape[0]
    b4 = b_ada.reshape(DEPTH, N_MOD, 1, D_MODEL)
    out = pl.pallas_call(
        _mod_kernel,
        out_shape=jax.ShapeDtypeStruct((DEPTH, N_MOD, nrow, D_MODEL), F32),
        grid=(DEPTH, N_MOD // MOD_PER_STEP),
        in_specs=[
            pl.BlockSpec((nrow, D_MODEL), lambda l, j: (0, 0)),
            pl.BlockSpec((None, D_MODEL, MOD_PER_STEP * D_MODEL), lambda l, j: (l, 0, j)),
            pl.BlockSpec((None, MOD_PER_STEP, 1, D_MODEL), lambda l, j: (l, j, 0, 0)),
        ],
        out_specs=pl.BlockSpec((None, MOD_PER_STEP, nrow, D_MODEL), lambda l, j: (l, j, 0, 0)),
        compiler_params=pltpu.CompilerParams(dimension_semantics=("arbitrary", "arbitrary"),
                                             vmem_limit_bytes=VMEM_LIMIT),
        name="adaln_modulation",
    )(c_all, w_ada, b4)
    return out.reshape(DEPTH, N_MOD, nrow, 1, D_MODEL)


def _relayout_kernel(wt_ref, o_ref):
    def put(col0, row0, ncol):
        for c in range(0, ncol, RELAYOUT_COLS):
            o_ref[:, col0 + c:col0 + c + RELAYOUT_COLS] = (
                wt_ref[row0 + c:row0 + c + RELAYOUT_COLS, :].T.astype(BF16))

    put(0, 0, REF_OFF_I)
    put(OFF_B, REF_OFF_B, REF_N_IN - REF_OFF_B)
    g = wt_ref[REF_OFF_I:REF_OFF_I + LANES, :].T
    lane = lax.broadcasted_iota(jnp.int32, g.shape, 1)
    o_ref[:, OFF_GI:OFF_GF] = jnp.where(lane < M_HEADS, g, 0.0).astype(BF16)
    o_ref[:, OFF_GF:N_Z] = jnp.where(lane < M_HEADS, pltpu.roll(g, LANES - M_HEADS, axis=1), 0.0).astype(BF16)


def _relayout_w_in(w_in):
    wt = jnp.swapaxes(w_in, 1, 2)
    return pl.pallas_call(
        _relayout_kernel,
        out_shape=jax.ShapeDtypeStruct((DEPTH, D_MODEL, N_Z), BF16),
        grid=(DEPTH,),
        in_specs=[pl.BlockSpec((None, REF_N_IN, D_MODEL), lambda l: (l, 0, 0))],
        out_specs=pl.BlockSpec((None, D_MODEL, N_Z), lambda l: (l, 0, 0)),
        compiler_params=pltpu.CompilerParams(dimension_semantics=("arbitrary",), vmem_limit_bytes=VMEM_LIMIT),
        name="relayout_w_in",
    )(wt)


def _mixer_part(r, *, nb, tm, seg, first_tile, lo, th):
    assert th == tm or nb == 1
    rows = nb * th
    fsl = slice(lo, lo + rows)
    tsl = slice(lo, lo + th)
    nblk = rows // BLK
    blk0 = lo // BLK
    nseg = BLK // seg
    v = types.SimpleNamespace(blk={})

    if first_tile is None:
        assert lo == 0
        c_in, n_in, m_in, cv_in = r.c0, r.n0, r.m0, r.cv0
    else:
        c_in, n_in, m_in, cv_in = r.c, r.n, r.m, r.cv

    def proj(off, width):
        return (jnp.dot(v.hb, r.win[:, off:off + width], preferred_element_type=F32)
                + r.bin[:, off:off + width])

    def norm_gates():
        if first_tile is not None and lo == 0:
            @pl.when(first_tile)
            def _():
                r.c[...] = r.c0[...]
                r.n[...] = r.n0[...]
                r.m[...] = r.m0[...]
                r.cv[...] = r.cv0[...]

        v.x3 = r.x[:, tsl, :]
        h3 = _rms_rows(v.x3) * (r.g1[...] * (1.0 + r.sc[...])) + r.sh[...]
        v.hb = h3.reshape(rows, D_MODEL).astype(BF16)
        v.gates = proj(OFF_GI, 2 * LANES)

    def q_proj():
        r.q_s[fsl, :] = proj(OFF_Q, M_WIDTH).astype(BF16)

    def gate_scalars():
        logi = v.gates[:, :LANES]
        logf = _log_sigmoid(v.gates[:, LANES:])
        rin = lax.broadcasted_iota(jnp.int32, (rows, LANES), 0) & (th - 1)
        bt = logf
        shift = 1
        while shift < th:
            bt = bt + jnp.where(rin >= shift, pltpu.roll(bt, shift, axis=0), 0.0)
            shift *= 2
        rr = logi - bt
        cm = rr
        shift = 1
        while shift < th:
            cm = jnp.maximum(cm, jnp.where(rin >= shift, pltpu.roll(cm, shift, axis=0), -jnp.inf))
            shift *= 2
        m0b = jnp.broadcast_to(m_in[...], (nb, th, LANES)).reshape(rows, LANES)
        cmx = jnp.maximum(m0b, cm)
        p = -cmx
        m = bt + cmx
        r.m[...] = m.reshape(nb, th, LANES)[:, th - 1:th, :]
        nchunk = rows // seg
        p3 = p.reshape(nchunk, seg, LANES)
        p_before = jnp.where(rin == 0, -m0b, pltpu.roll(p, 1, axis=0)).reshape(nchunk, seg, LANES)[:, 0:1, :]
        r.p_s[fsl, :] = p
        r.g_s[fsl, :] = jnp.exp(p3 - p_before).reshape(rows, LANES)
        r.we_s[fsl, :] = jnp.exp(p3[:, seg - 1:seg, :] + rr.reshape(nchunk, seg, LANES)).reshape(rows, LANES)
        r.em_s[fsl, :] = jnp.exp(-m)
        for bi in range(nblk):
            r.rt_s[:, lo + bi * BLK:lo + (bi + 1) * BLK] = rr[bi * BLK:(bi + 1) * BLK, :].T

    def k_proj():
        r.k_s[fsl, :] = (proj(OFF_K, M_WIDTH) * (M_DK ** -0.5)).astype(BF16)

    def v_proj():
        r.v_s[fsl, :] = proj(OFF_V, M_WIDTH).astype(BF16)

    def o_proj():
        r.og_s[fsl, :] = jax.nn.sigmoid(proj(OFF_O, M_WIDTH))

    def block_diag(a, rowseg):
        if nseg == 1:
            return a
        zero = jnp.zeros_like(a)
        return jnp.concatenate([jnp.where(rowseg == j, a, zero) for j in range(nseg)], axis=1)

    def block_consts(bi):
        ri = lax.broadcasted_iota(jnp.int32, (BLK, BLK), 0)
        ci = lax.broadcasted_iota(jnp.int32, (BLK, BLK), 1)
        sh = seg.bit_length() - 1
        rowseg = ri >> sh
        mask = ci <= ri
        if nseg > 1:
            mask = mask & (rowseg == (ci >> sh))
        r0 = bi * BLK
        rsl = slice(r0, r0 + BLK)
        seq0 = r0 // tm
        return rowseg, mask, rsl, slice(seq0, seq0 + nseg)

    def block_scores(bi):
        rowseg, mask, rsl, _ = block_consts(bi)
        hsl = [slice(h * M_DK, (h + 1) * M_DK) for h in range(M_HEADS)]
        q = [r.q_s[rsl, hs] for hs in hsl]
        k = [r.k_s[rsl, hs] for hs in hsl]
        vv = [r.v_s[rsl, hs] for hs in hsl]
        s = [lax.dot_general(q[h], k[h], (((1,), (1,)), ((), ())), preferred_element_type=F32)
             for h in range(M_HEADS)]
        sw = [s[h] * jnp.exp(jnp.where(mask, r.p_s[rsl, h:h + 1] + r.rt_s[h:h + 1, rsl], -jnp.inf))
              for h in range(M_HEADS)]
        kw = [k[h].astype(F32) * r.we_s[rsl, h:h + 1] for h in range(M_HEADS)]
        blk = types.SimpleNamespace(q=q)
        blk.rowsum = [jnp.sum(sw[h], axis=-1, keepdims=True) for h in range(M_HEADS)]
        blk.intra = [jnp.dot(sw[h].astype(BF16), vv[h], preferred_element_type=F32) for h in range(M_HEADS)]
        blk.upd = [lax.dot_general(block_diag(kw[h].astype(BF16), rowseg), vv[h], (((0,), (0,)), ((), ())),
                                   preferred_element_type=F32) for h in range(M_HEADS)]
        blk.ksum = [jnp.sum(kw[h].reshape(nseg, seg, M_DK), axis=1, keepdims=True) for h in range(M_HEADS)]
        v.blk[bi] = blk

    def block_state(bi):
        rowseg, _, rsl, ssl = block_consts(bi)
        blk = v.blk.pop(bi)
        g_blk = r.g_s[rsl, :].reshape(nseg, seg, LANES)
        heads = range(M_HEADS)
        c0 = [c_in[ssl, h] for h in heads]
        n0 = [n_in[ssl, h:h + 1, :] for h in heads]
        inter = [jnp.dot(block_diag(blk.q[h], rowseg), c0[h].astype(BF16).reshape(nseg * M_DK, M_DK),
                         preferred_element_type=F32) for h in heads]
        qn = [jnp.sum(blk.q[h].astype(F32) * jnp.broadcast_to(n0[h], (nseg, seg, M_DK)).reshape(BLK, M_DK),
                      axis=-1, keepdims=True) for h in heads]
        gc = [r.g_s[rsl, h:h + 1] for h in heads]
        num = [gc[h] * inter[h] + blk.intra[h] for h in heads]
        den = [gc[h] * qn[h] + blk.rowsum[h] for h in heads]
        hh = [num[h] * (1.0 / jnp.maximum(jnp.abs(den[h]), r.em_s[rsl, h:h + 1])) for h in heads]
        for h in heads:
            r.hn_s[rsl, h * M_DK:(h + 1) * M_DK] = _rms_rows(hh[h])
        for h in heads:
            decay = g_blk[:, seg - 1:seg, h:h + 1]
            r.c[ssl, h] = decay * c0[h] + blk.upd[h].reshape(nseg, M_DK, M_DK)
            r.n[ssl, h:h + 1, :] = decay * n0[h] + blk.ksum[h]

    def conv_inputs():
        if lo == 0:
            r.ubuf_s[:, 6:8, :] = cv_in[...]
        u = proj(OFF_C, S_WIDTH) * proj(OFF_X, S_WIDTH)
        r.ubuf_s[:, 8 + lo:8 + lo + th, :] = u.reshape(nb, th, S_WIDTH)
        r.cv[...] = r.ubuf_s[:, lo + th + 6:lo + th + 8, :]

    def conv_outputs():
        bg = proj(OFF_B, S_WIDTH)
        cw = r.cw[...]
        yc = (cw[0:1] * r.ubuf_s[:, 6 + lo:6 + lo + th, :] + cw[1:2] * r.ubuf_s[:, 7 + lo:7 + lo + th, :]
              + cw[2:3] * r.ubuf_s[:, 8 + lo:8 + lo + th, :])
        ysv = (bg.reshape(nb, th, S_WIDTH) * yc).reshape(rows, S_WIDTH)
        for grp in range(S_GROUPS):
            cs = slice(M_WIDTH + grp * LANES, M_WIDTH + (grp + 1) * LANES)
            r.mix_s[fsl, cs] = (_rms_rows(ysv[:, grp * LANES:(grp + 1) * LANES]) * r.gmix[:, cs]).astype(BF16)

    def conv_half_out():
        v.y_conv = jnp.dot(r.mix_s[fsl, M_WIDTH:], r.wout[M_WIDTH:, :], preferred_element_type=F32)

    def finish():
        r.mix_s[fsl, :M_WIDTH] = (r.hn_s[fsl, :] * r.og_s[fsl, :] * r.gmix[:, :M_WIDTH]).astype(BF16)
        y = v.y_conv + jnp.dot(r.mix_s[fsl, :M_WIDTH], r.wout[:M_WIDTH, :], preferred_element_type=F32)
        r.xo[:, tsl, :] = v.x3 + r.gt[...] * y.reshape(nb, th, D_MODEL)

    blocks = [functools.partial(block_scores, blk0)]
    for bi in range(blk0, blk0 + nblk):
        if bi + 1 < blk0 + nblk:
            blocks.append(functools.partial(block_scores, bi + 1))
        blocks.append(functools.partial(block_state, bi))
    return types.SimpleNamespace(
        pre=[norm_gates, q_proj, gate_scalars, k_proj, v_proj], blocks=blocks,
        fillers=[o_proj, conv_inputs, conv_outputs, conv_half_out], finish=finish)


def _spread(chain, others):
    out, done = [], 0
    for i, step in enumerate(chain):
        out.append(step)
        want = ((i + 1) * len(others)) // len(chain)
        out.extend(others[done:want])
        done = want
    return out


def _mixer_steps(r, *, nb, tm, seg, first_tile, parts):
    th = tm // parts
    part = [_mixer_part(r, nb=nb, tm=tm, seg=seg, first_tile=first_tile, lo=i * th, th=th) for i in range(parts)]
    if parts == 1:
        a = part[0]
        return a.pre + _spread(a.blocks, a.fillers) + [a.finish]
    a, b = part
    return (a.pre
            + _spread(a.blocks, b.pre + a.fillers[:2])
            + _spread(b.blocks, a.fillers[2:] + [a.finish] + b.fillers[:2])
            + b.fillers[2:] + [b.finish])


_MIXER_REFS = (
    "x", "sh", "sc", "gt", "g1", "win", "bin", "cw", "gmix", "wout", "c0", "n0", "m0", "cv0",
    "xo", "c", "n", "m", "cv",
    "q_s", "k_s", "v_s", "og_s", "hn_s", "mix_s", "ubuf_s", "p_s", "g_s", "we_s", "em_s", "rt_s",
)


def _mixer_kernel(*refs, nb, tm, seg, nt, parts, n_cast):
    n_in, n_out = _MIXER_REFS.index("xo"), _MIXER_REFS.index("q_s")
    cast_in = refs[n_in:n_in + n_cast]
    cast_out = refs[n_out + n_cast:n_out + 2 * n_cast]
    refs = refs[:n_in] + refs[n_in + n_cast:n_out + n_cast] + refs[n_out + 2 * n_cast:]
    r = types.SimpleNamespace(**dict(zip(_MIXER_REFS, refs, strict=True)))
    first_tile = None if nt == 1 else pl.program_id(1) == 0
    for step in _mixer_steps(r, nb=nb, tm=tm, seg=seg, first_tile=first_tile, parts=parts):
        step()
    for src_ref, dst_ref in zip(cast_in, cast_out):
        dst_ref[...] = src_ref[...].astype(BF16)


def _mixer(x, mod, mod_row0, layer, w, state, *, nb, tm, parts, name, cast=()):
    bsz, t_len, _ = x.shape
    rows = nb * tm
    seg = min(BLK, tm // parts)
    assert (rows // parts) % BLK == 0 and bsz % nb == 0 and t_len % tm == 0 and mod_row0 % nb == 0
    assert tm & (tm - 1) == 0 and (nb == 1 or tm == t_len)
    c0, n0, m0, cv0, st_layer = state
    grid = (bsz // nb, t_len // tm)
    mrow = mod_row0 // nb

    def mod_spec(j):
        return pl.BlockSpec((None, None, nb, 1, D_MODEL), lambda b, t: (layer, j, mrow + b, 0, 0))

    def wspec(shape, resident=False):
        nd = len(shape)
        mode = {"pipeline_mode": pl.Buffered(1)} if resident else {}
        return pl.BlockSpec((None,) + shape, lambda b, t: (layer,) + (0,) * nd, **mode)

    def sspec(shape):
        nd = len(shape)
        return pl.BlockSpec((None, nb) + shape, lambda b, t: (st_layer, b) + (0,) * nd)

    def ospec(shape):
        nd = len(shape)
        return pl.BlockSpec((nb,) + shape, lambda b, t: (b,) + (0,) * nd)

    state_shapes = ((M_HEADS, M_DK, M_DK), (M_HEADS, M_DK), (1, LANES), (CONV_W - 1, S_WIDTH))
    in_specs = [
        pl.BlockSpec((nb, tm, D_MODEL), lambda b, t: (b, t, 0)),
        mod_spec(0), mod_spec(1), mod_spec(2),
        wspec((1, D_MODEL)),
        wspec((D_MODEL, N_Z), True),
        wspec((1, N_Z)),
        wspec((CONV_W, S_WIDTH)),
        wspec((1, D_MODEL)),
        wspec((D_MODEL, D_MODEL), True),
    ] + [sspec(s) for s in state_shapes]
    out_shape = (jax.ShapeDtypeStruct(x.shape, F32),) + tuple(
        jax.ShapeDtypeStruct((bsz,) + s, F32) for s in state_shapes)
    out_specs = (pl.BlockSpec((nb, tm, D_MODEL), lambda b, t: (b, t, 0)),) + tuple(ospec(s) for s in state_shapes)
    nstep = grid[0] * grid[1]
    for a in cast:
        slab = a.shape[1] // nstep
        assert a.shape[1] % nstep == 0 and slab % 16 == 0
        in_specs.append(pl.BlockSpec((None, slab, a.shape[2]), lambda b, t: (layer, b * grid[1] + t, 0)))
        out_shape += (jax.ShapeDtypeStruct(a.shape[1:], BF16),)
        out_specs += (pl.BlockSpec((slab, a.shape[2]), lambda b, t: (b * grid[1] + t, 0)),)
    scratch = [
        pltpu.VMEM((rows, M_WIDTH), BF16),
        pltpu.VMEM((rows, M_WIDTH), BF16),
        pltpu.VMEM((rows, M_WIDTH), BF16),
        pltpu.VMEM((rows, M_WIDTH), F32),
        pltpu.VMEM((rows, M_WIDTH), F32),
        pltpu.VMEM((rows, D_MODEL), BF16),
        pltpu.VMEM((nb, tm + 8, S_WIDTH), F32),
        pltpu.VMEM((rows, LANES), F32),
        pltpu.VMEM((rows, LANES), F32),
        pltpu.VMEM((rows, LANES), F32),
        pltpu.VMEM((rows, LANES), F32),
        pltpu.VMEM((LANES, rows), F32),
    ]
    kern = functools.partial(_mixer_kernel, nb=nb, tm=tm, seg=seg, nt=t_len // tm, parts=parts, n_cast=len(cast))
    return pl.pallas_call(
        kern, out_shape=out_shape, grid=grid, in_specs=in_specs, out_specs=out_specs,
        scratch_shapes=scratch,
        compiler_params=pltpu.CompilerParams(dimension_semantics=("arbitrary", "arbitrary"),
                                             vmem_limit_bytes=VMEM_LIMIT),
        name=name,
    )(x, mod, mod, mod, w["g1"], w["w_in"], w["b_in"], w["conv_w"], w["g_mix"], w["w_out"],
      c0, n0, m0, cv0, *cast)


def _mlp_tile(x_ref, sh_ref, sc_ref, gt_ref, g2_ref, wup_ref, wdn_ref, gf_ref, xo_ref, *, final):
    nb, tm, _ = x_ref.shape
    rows = nb * tm
    x3 = x_ref[...]
    h3 = _rms_rows(x3) * (g2_ref[...] * (1.0 + sc_ref[...])) + sh_ref[...]
    hb = h3.reshape(rows, D_MODEL).astype(BF16)
    acc = jnp.zeros((rows, D_MODEL), F32)
    for c in range(D_FF // FF_CHUNK):
        lo = c * FF_CHUNK
        up = jnp.dot(hb, wup_ref[:, lo:lo + FF_CHUNK], preferred_element_type=F32)
        act = jnp.square(jnp.maximum(up, 0.0)).astype(BF16)
        acc = acc + jnp.dot(act, wdn_ref[lo:lo + FF_CHUNK, :], preferred_element_type=F32)
    xn = x3 + gt_ref[...] * acc.reshape(nb, tm, D_MODEL)
    if final:
        xn = _rms_rows(xn) * gf_ref[...]
    xo_ref[...] = xn


def _mlp_kernel(xp_ref, shp_ref, scp_ref, gtp_ref, xs_ref, shs_ref, scs_ref, gts_ref, g2_ref, wup_ref, wdn_ref,
                gf_ref, xop_ref, xos_ref, *, n_prompt, final):
    s = pl.program_id(0)
    shared = (g2_ref, wup_ref, wdn_ref, gf_ref)

    @pl.when(s < n_prompt)
    def _():
        _mlp_tile(xp_ref, shp_ref, scp_ref, gtp_ref, *shared, xop_ref, final=final)

    @pl.when(s == n_prompt)
    def _():
        _mlp_tile(xs_ref, shs_ref, scs_ref, gts_ref, *shared, xos_ref, final=final)


def _mlp(xp, xs, mod, mod_row0, layer, w, w_up, w_down, g_final, *, tm, final, name):
    bp, t_len, _ = xp.shape
    bs, ss, _ = xs.shape
    assert t_len % tm == 0
    nt = t_len // tm
    n_prompt = bp * nt

    def ptile(s):
        return jnp.minimum(s, n_prompt - 1)

    def pmod(j):
        return pl.BlockSpec((None, None, 1, 1, D_MODEL), lambda s: (layer, j, mod_row0 + ptile(s) // nt, 0, 0))

    def smod(j):
        return pl.BlockSpec((None, None, bs, 1, D_MODEL), lambda s: (layer, j, 0, 0, 0))

    xp_spec = pl.BlockSpec((1, tm, D_MODEL), lambda s: (ptile(s) // nt, ptile(s) % nt, 0))
    in_specs = [
        xp_spec, pmod(3), pmod(4), pmod(5),
        pl.BlockSpec((bs, ss, D_MODEL), lambda s: (0, 0, 0), pipeline_mode=pl.Buffered(1)), smod(3), smod(4), smod(5),
        pl.BlockSpec((None, 1, D_MODEL), lambda s: (layer, 0, 0)),
        pl.BlockSpec((D_MODEL, D_FF), lambda s: (0, 0), pipeline_mode=pl.Buffered(1)),
        pl.BlockSpec((D_FF, D_MODEL), lambda s: (0, 0), pipeline_mode=pl.Buffered(1)),
        pl.BlockSpec((1, D_MODEL), lambda s: (0, 0)),
    ]
    return pl.pallas_call(
        functools.partial(_mlp_kernel, n_prompt=n_prompt, final=final),
        out_shape=(jax.ShapeDtypeStruct(xp.shape, F32), jax.ShapeDtypeStruct(xs.shape, F32)),
        grid=(n_prompt + 1,), in_specs=in_specs,
        out_specs=(xp_spec, pl.BlockSpec((bs, ss, D_MODEL), lambda s: (0, 0, 0))),
        compiler_params=pltpu.CompilerParams(dimension_semantics=("arbitrary",), vmem_limit_bytes=VMEM_LIMIT),
        name=name,
    )(xp, mod, mod, mod, xs, mod, mod, mod, w["g2"], w_up, w_down, g_final)


def kernel(x_prompt, x_sample, c_prompt, c_sample, state_C, state_n, state_m, state_conv, w_ada, b_ada,
           g_norm1, w_in, b_in, conv_w, g_mix_out, w_out, g_norm2, w_up, w_down, g_final):
    bp, sp, _ = x_prompt.shape
    bs, ss, _ = x_sample.shape

    def relayout_bias(a):
        gi = a[..., REF_OFF_I:REF_OFF_F]
        gf = a[..., REF_OFF_F:REF_OFF_B]
        pad = [(0, 0)] * (a.ndim - 1) + [(0, LANES - M_HEADS)]
        return jnp.concatenate([a[..., :REF_OFF_I], a[..., REF_OFF_B:], jnp.pad(gi, pad), jnp.pad(gf, pad)], axis=-1)

    w = {
        "g1": g_norm1.reshape(DEPTH, 1, D_MODEL),
        "w_in": _relayout_w_in(w_in),
        "b_in": relayout_bias(b_in).reshape(DEPTH, 1, N_Z),
        "conv_w": conv_w,
        "g_mix": g_mix_out.reshape(DEPTH, 1, D_MODEL),
        "w_out": w_out,
        "g2": g_norm2.reshape(DEPTH, 1, D_MODEL),
    }
    gfin = g_final.reshape(1, D_MODEL)

    mod = _modulation(jnp.concatenate([c_sample, c_prompt], axis=0), w_ada, b_ada)

    zero_state = (jnp.zeros((1, bp, M_HEADS, M_DK, M_DK), F32), jnp.zeros((1, bp, M_HEADS, M_DK), F32),
                  jnp.zeros((1, bp, 1, LANES), F32), jnp.zeros((1, bp, CONV_W - 1, S_WIDTH), F32))
    m_pad = jnp.pad(state_m, ((0, 0), (0, 0), (0, LANES - M_HEADS))).reshape(DEPTH, bs, 1, LANES)

    tm_p = 1024
    nb_s = 16
    xp, xs = x_prompt, x_sample
    outs = [[] for _ in range(8)]
    for l in range(DEPTH):
        last = l == DEPTH - 1
        xp, c1, n1, m1, cv1, wu, wd = _mixer(xp, mod, bs, l, w, zero_state + (0,), nb=1, tm=tm_p, parts=2,
                                             name=f"mixer_prompt_{l}", cast=(w_up, w_down))
        xs, c2, n2, m2, cv2 = _mixer(xs, mod, 0, l, w, (state_C, state_n, m_pad, state_conv, l),
                                     nb=nb_s, tm=ss, parts=1, name=f"mixer_sample_{l}")
        for lst, val in zip(outs, (c1, n1, m1[:, 0, :M_HEADS], cv1, c2, n2, m2[:, 0, :M_HEADS], cv2)):
            lst.append(val)
        xp, xs = _mlp(xp, xs, mod, bs, l, w, wu, wd, gfin, tm=tm_p, final=last, name=f"mlp_{l}")
    return (xp, xs) + tuple(jnp.stack(o) for o in outs)
```

```python
import functools
import types

import jax
import jax.numpy as jnp
from jax import lax
from jax.experimental import pallas as pl
from jax.experimental.pallas import tpu as pltpu

F32 = jnp.float32
BF16 = jnp.bfloat16

D_MODEL = 1024
DEPTH = 4
M_HEADS = 4
M_DK = 128
M_WIDTH = M_HEADS * M_DK
S_WIDTH = 512
S_GROUPS = 4
CONV_W = 3
D_FF = 4 * D_MODEL
N_MOD = 6
EPS = 1e-6

REF_OFF_I = 4 * M_WIDTH
REF_OFF_F = REF_OFF_I + M_HEADS
REF_OFF_B = REF_OFF_F + M_HEADS
REF_N_IN = REF_OFF_B + 3 * S_WIDTH

LANES = 128
SUBLANES = 8
BLK = 128
CONV_ROW0 = SUBLANES
OFF_Q, OFF_K, OFF_V, OFF_O = 0, 512, 1024, 1536
OFF_B, OFF_C, OFF_X = 2048, 2560, 3072
OFF_GI, OFF_GF = 3584, 3712
N_Z = 3840
FF_CHUNK = 1024
RELAYOUT_COLS = 256
VMEM_LIMIT = 58 * 1024 * 1024
PROMPT_TILE_ROWS = 1024
SAMPLE_SEQS_PER_STEP = 16


def _rms_rows(x):
    return x * lax.rsqrt(jnp.mean(x * x, axis=-1, keepdims=True) + EPS)


def _log_sigmoid(x):
    return jnp.minimum(x, 0.0) - jnp.log1p(jnp.exp(-jnp.abs(x)))


MOD_PER_STEP = 2


def _mod_kernel(c_ref, w_ref, b_ref, o_ref):
    c = c_ref[...]
    a = (c * jax.nn.sigmoid(c)).astype(BF16)
    res = jnp.dot(a, w_ref[...].astype(BF16), preferred_element_type=F32)
    for k in range(MOD_PER_STEP):
        o_ref[k] = res[:, k * D_MODEL:(k + 1) * D_MODEL] + b_ref[k]


def _modulation(c_all, w_ada, b_ada):
    nrow = c_all.shape[0]
    b4 = b_ada.reshape(DEPTH, N_MOD, 1, D_MODEL)
    out = pl.pallas_call(
        _mod_kernel,
        out_shape=jax.ShapeDtypeStruct((DEPTH, N_MOD, nrow, D_MODEL), F32),
        grid=(DEPTH, N_MOD // MOD_PER_STEP),
        in_specs=[
            pl.BlockSpec((nrow, D_MODEL), lambda l, j: (0, 0)),
            pl.BlockSpec((None, D_MODEL, MOD_PER_STEP * D_MODEL), lambda l, j: (l, 0, j)),
            pl.BlockSpec((None, MOD_PER_STEP, 1, D_MODEL), lambda l, j: (l, j, 0, 0)),
        ],
        out_specs=pl.BlockSpec((None, MOD_PER_STEP, nrow, D_MODEL), lambda l, j: (l, j, 0, 0)),
        compiler_params=pltpu.CompilerParams(dimension_semantics=("arbitrary", "arbitrary"),
                                             vmem_limit_bytes=VMEM_LIMIT),
        name="adaln_modulation",
    )(c_all, w_ada, b4)
    return out.reshape(DEPTH, N_MOD, nrow, 1, D_MODEL)


def _relayout_kernel(wt_ref, o_ref):
    def put(col0, row0, ncol):
        for c in range(0, ncol, RELAYOUT_COLS):
            o_ref[:, col0 + c:col0 + c + RELAYOUT_COLS] = (
                wt_ref[row0 + c:row0 + c + RELAYOUT_COLS, :].T.astype(BF16))

    put(0, 0, REF_OFF_I)
    put(OFF_B, REF_OFF_B, REF_N_IN - REF_OFF_B)
    g = wt_ref[REF_OFF_I:REF_OFF_I + LANES, :].T
    lane = lax.broadcasted_iota(jnp.int32, g.shape, 1)
    o_ref[:, OFF_GI:OFF_GF] = jnp.where(lane < M_HEADS, g, 0.0).astype(BF16)
    o_ref[:, OFF_GF:N_Z] = jnp.where(lane < M_HEADS, pltpu.roll(g, LANES - M_HEADS, axis=1), 0.0).astype(BF16)


def _relayout_w_in(w_in):
    wt = jnp.swapaxes(w_in, 1, 2)
    return pl.pallas_call(
        _relayout_kernel,
        out_shape=jax.ShapeDtypeStruct((DEPTH, D_MODEL, N_Z), BF16),
        grid=(DEPTH,),
        in_specs=[pl.BlockSpec((None, REF_N_IN, D_MODEL), lambda l: (l, 0, 0))],
        out_specs=pl.BlockSpec((None, D_MODEL, N_Z), lambda l: (l, 0, 0)),
        compiler_params=pltpu.CompilerParams(dimension_semantics=("arbitrary",), vmem_limit_bytes=VMEM_LIMIT),
        name="relayout_w_in",
    )(wt)


def _mixer_part(r, *, nb, tm, seg, first_tile, lo, th):
    assert th == tm or nb == 1
    rows = nb * th
    fsl = slice(lo, lo + rows)
    tsl = slice(lo, lo + th)
    nblk = rows // BLK
    blk0 = lo // BLK
    nseg = BLK // seg
    v = types.SimpleNamespace(blk={})

    if first_tile is None:
        assert lo == 0
        c_in, n_in, m_in, cv_in = r.c0, r.n0, r.m0, r.cv0
    else:
        c_in, n_in, m_in, cv_in = r.c, r.n, r.m, r.cv

    def proj(off, width):
        return (jnp.dot(v.hb, r.win[:, off:off + width], preferred_element_type=F32)
                + r.bin[:, off:off + width])

    def norm_gates():
        if first_tile is not None and lo == 0:
            @pl.when(first_tile)
            def _():
                r.c[...] = r.c0[...]
                r.n[...] = r.n0[...]
                r.m[...] = r.m0[...]
                r.cv[...] = r.cv0[...]

        v.x3 = r.x[:, tsl, :]
        h3 = _rms_rows(v.x3) * (r.g1[...] * (1.0 + r.sc[...])) + r.sh[...]
        v.hb = h3.reshape(rows, D_MODEL).astype(BF16)
        v.gates = proj(OFF_GI, 2 * LANES)

    def q_proj():
        r.q_s[fsl, :] = proj(OFF_Q, M_WIDTH).astype(BF16)

    def gate_scalars():
        logi = v.gates[:, :LANES]
        logf = _log_sigmoid(v.gates[:, LANES:])
        rin = lax.broadcasted_iota(jnp.int32, (rows, LANES), 0) & (th - 1)
        bt = logf
        shift = 1
        while shift < th:
            bt = bt + jnp.where(rin >= shift, pltpu.roll(bt, shift, axis=0), 0.0)
            shift *= 2
        rr = logi - bt
        cm = rr
        shift = 1
        while shift < th:
            cm = jnp.maximum(cm, jnp.where(rin >= shift, pltpu.roll(cm, shift, axis=0), -jnp.inf))
            shift *= 2
        m0b = jnp.broadcast_to(m_in[...], (nb, th, LANES)).reshape(rows, LANES)
        cmx = jnp.maximum(m0b, cm)
        p = -cmx
        m = bt + cmx
        r.m[...] = m.reshape(nb, th, LANES)[:, th - 1:th, :]
        nchunk = rows // seg
        p3 = p.reshape(nchunk, seg, LANES)
        p_before = jnp.where(rin == 0, -m0b, pltpu.roll(p, 1, axis=0)).reshape(nchunk, seg, LANES)[:, 0:1, :]
        r.p_s[fsl, :] = p
        r.g_s[fsl, :] = jnp.exp(p3 - p_before).reshape(rows, LANES)
        r.we_s[fsl, :] = jnp.exp(p3[:, seg - 1:seg, :] + rr.reshape(nchunk, seg, LANES)).reshape(rows, LANES)
        r.em_s[fsl, :] = jnp.exp(-m)
        for bi in range(nblk):
            r.rt_s[:, lo + bi * BLK:lo + (bi + 1) * BLK] = rr[bi * BLK:(bi + 1) * BLK, :].T

    def k_proj():
        r.k_s[fsl, :] = (proj(OFF_K, M_WIDTH) * (M_DK ** -0.5)).astype(BF16)

    def v_proj():
        r.v_s[fsl, :] = proj(OFF_V, M_WIDTH).astype(BF16)

    def o_proj():
        r.og_s[fsl, :] = jax.nn.sigmoid(proj(OFF_O, M_WIDTH))

    def block_diag(a, rowseg):
        if nseg == 1:
            return a
        zero = jnp.zeros_like(a)
        return jnp.concatenate([jnp.where(rowseg == j, a, zero) for j in range(nseg)], axis=1)

    def block_consts(bi):
        ri = lax.broadcasted_iota(jnp.int32, (BLK, BLK), 0)
        ci = lax.broadcasted_iota(jnp.int32, (BLK, BLK), 1)
        sh = seg.bit_length() - 1
        rowseg = ri >> sh
        mask = ci <= ri
        if nseg > 1:
            mask = mask & (rowseg == (ci >> sh))
        r0 = bi * BLK
        rsl = slice(r0, r0 + BLK)
        seq0 = r0 // tm
        return rowseg, mask, rsl, slice(seq0, seq0 + nseg)

    def block_scores(bi):
        rowseg, mask, rsl, _ = block_consts(bi)
        hsl = [slice(h * M_DK, (h + 1) * M_DK) for h in range(M_HEADS)]
        q = [r.q_s[rsl, hs] for hs in hsl]
        k = [r.k_s[rsl, hs] for hs in hsl]
        vv = [r.v_s[rsl, hs] for hs in hsl]
        s = [lax.dot_general(q[h], k[h], (((1,), (1,)), ((), ())), preferred_element_type=F32)
             for h in range(M_HEADS)]
        sw = [s[h] * jnp.exp(jnp.where(mask, r.p_s[rsl, h:h + 1] + r.rt_s[h:h + 1, rsl], -jnp.inf))
              for h in range(M_HEADS)]
        kw = [k[h].astype(F32) * r.we_s[rsl, h:h + 1] for h in range(M_HEADS)]
        blk = types.SimpleNamespace(q=q)
        blk.rowsum = [jnp.sum(sw[h], axis=-1, keepdims=True) for h in range(M_HEADS)]
        blk.intra = [jnp.dot(sw[h].astype(BF16), vv[h], preferred_element_type=F32) for h in range(M_HEADS)]
        blk.upd = [lax.dot_general(block_diag(kw[h].astype(BF16), rowseg), vv[h], (((0,), (0,)), ((), ())),
                                   preferred_element_type=F32) for h in range(M_HEADS)]
        blk.ksum = [jnp.sum(kw[h].reshape(nseg, seg, M_DK), axis=1, keepdims=True) for h in range(M_HEADS)]
        v.blk[bi] = blk

    def block_state(bi):
        rowseg, _, rsl, ssl = block_consts(bi)
        blk = v.blk.pop(bi)
        g_blk = r.g_s[rsl, :].reshape(nseg, seg, LANES)
        heads = range(M_HEADS)
        c0 = [c_in[ssl, h] for h in heads]
        n0 = [n_in[ssl, h:h + 1, :] for h in heads]
        inter = [jnp.dot(block_diag(blk.q[h], rowseg), c0[h].astype(BF16).reshape(nseg * M_DK, M_DK),
                         preferred_element_type=F32) for h in heads]
        qn = [jnp.sum(blk.q[h].astype(F32) * jnp.broadcast_to(n0[h], (nseg, seg, M_DK)).reshape(BLK, M_DK),
                      axis=-1, keepdims=True) for h in heads]
        gc = [r.g_s[rsl, h:h + 1] for h in heads]
        num = [gc[h] * inter[h] + blk.intra[h] for h in heads]
        den = [gc[h] * qn[h] + blk.rowsum[h] for h in heads]
        hh = [num[h] * (1.0 / jnp.maximum(jnp.abs(den[h]), r.em_s[rsl, h:h + 1])) for h in heads]
        for h in heads:
            r.hn_s[rsl, h * M_DK:(h + 1) * M_DK] = _rms_rows(hh[h])
        for h in heads:
            decay = g_blk[:, seg - 1:seg, h:h + 1]
            r.c[ssl, h] = decay * c0[h] + blk.upd[h].reshape(nseg, M_DK, M_DK)
            r.n[ssl, h:h + 1, :] = decay * n0[h] + blk.ksum[h]

    def conv_inputs():
        if lo == 0:
            r.ubuf_s[:, CONV_ROW0 - CONV_W + 1:CONV_ROW0, :] = cv_in[...]
        u = proj(OFF_C, S_WIDTH) * proj(OFF_X, S_WIDTH)
        r.ubuf_s[:, CONV_ROW0 + lo:CONV_ROW0 + lo + th, :] = u.reshape(nb, th, S_WIDTH)
        r.cv[...] = r.ubuf_s[:, CONV_ROW0 + lo + th - CONV_W + 1:CONV_ROW0 + lo + th, :]

    def conv_outputs():
        bg = proj(OFF_B, S_WIDTH)
        cw = r.cw[...]
        first = CONV_ROW0 + lo - CONV_W + 1
        yc = cw[0:1] * r.ubuf_s[:, first:first + th, :]
        for j in range(1, CONV_W):
            yc = yc + cw[j:j + 1] * r.ubuf_s[:, first + j:first + j + th, :]
        ysv = (bg.reshape(nb, th, S_WIDTH) * yc).reshape(rows, S_WIDTH)
        for grp in range(S_GROUPS):
            cs = slice(M_WIDTH + grp * LANES, M_WIDTH + (grp + 1) * LANES)
            r.mix_s[fsl, cs] = (_rms_rows(ysv[:, grp * LANES:(grp + 1) * LANES]) * r.gmix[:, cs]).astype(BF16)

    def conv_half_out():
        v.y_conv = jnp.dot(r.mix_s[fsl, M_WIDTH:], r.wout[M_WIDTH:, :], preferred_element_type=F32)

    def finish():
        r.mix_s[fsl, :M_WIDTH] = (r.hn_s[fsl, :] * r.og_s[fsl, :] * r.gmix[:, :M_WIDTH]).astype(BF16)
        y = v.y_conv + jnp.dot(r.mix_s[fsl, :M_WIDTH], r.wout[:M_WIDTH, :], preferred_element_type=F32)
        r.xo[:, tsl, :] = v.x3 + r.gt[...] * y.reshape(nb, th, D_MODEL)

    blocks = [functools.partial(block_scores, blk0)]
    for bi in range(blk0, blk0 + nblk):
        if bi + 1 < blk0 + nblk:
            blocks.append(functools.partial(block_scores, bi + 1))
        blocks.append(functools.partial(block_state, bi))
    return types.SimpleNamespace(
        pre=[norm_gates, q_proj, gate_scalars, k_proj, v_proj], blocks=blocks,
        fillers=[o_proj, conv_inputs, conv_outputs, conv_half_out], finish=finish)


def _spread(chain, others):
    out, done = [], 0
    for i, step in enumerate(chain):
        out.append(step)
        want = ((i + 1) * len(others)) // len(chain)
        out.extend(others[done:want])
        done = want
    return out


def _mixer_steps(r, *, nb, tm, seg, first_tile, parts):
    th = tm // parts
    part = [_mixer_part(r, nb=nb, tm=tm, seg=seg, first_tile=first_tile, lo=i * th, th=th) for i in range(parts)]
    if parts == 1:
        a = part[0]
        return a.pre + _spread(a.blocks, a.fillers) + [a.finish]
    a, b = part
    return (a.pre
            + _spread(a.blocks, b.pre + a.fillers[:2])
            + _spread(b.blocks, a.fillers[2:] + [a.finish] + b.fillers[:2])
            + b.fillers[2:] + [b.finish])


_MIXER_REFS = (
    "x", "sh", "sc", "gt", "g1", "win", "bin", "cw", "gmix", "wout", "c0", "n0", "m0", "cv0",
    "xo", "c", "n", "m", "cv",
    "q_s", "k_s", "v_s", "og_s", "hn_s", "mix_s", "ubuf_s", "p_s", "g_s", "we_s", "em_s", "rt_s",
)


def _mixer_kernel(*refs, nb, tm, seg, nt, parts, n_cast):
    n_in, n_out = _MIXER_REFS.index("xo"), _MIXER_REFS.index("q_s")
    cast_in = refs[n_in:n_in + n_cast]
    cast_out = refs[n_out + n_cast:n_out + 2 * n_cast]
    refs = refs[:n_in] + refs[n_in + n_cast:n_out + n_cast] + refs[n_out + 2 * n_cast:]
    r = types.SimpleNamespace(**dict(zip(_MIXER_REFS, refs, strict=True)))
    first_tile = None if nt == 1 else pl.program_id(1) == 0
    for step in _mixer_steps(r, nb=nb, tm=tm, seg=seg, first_tile=first_tile, parts=parts):
        step()
    for src_ref, dst_ref in zip(cast_in, cast_out):
        dst_ref[...] = src_ref[...].astype(BF16)


def _mixer(x, mod, mod_row0, layer, w, state, *, nb, tm, parts, name, cast=()):
    bsz, t_len, _ = x.shape
    rows = nb * tm
    seg = min(BLK, tm // parts)
    assert (rows // parts) % BLK == 0 and bsz % nb == 0 and t_len % tm == 0 and mod_row0 % nb == 0
    assert tm & (tm - 1) == 0 and (nb == 1 or tm == t_len)
    c0, n0, m0, cv0, st_layer = state
    grid = (bsz // nb, t_len // tm)
    mrow = mod_row0 // nb

    def mod_spec(j):
        return pl.BlockSpec((None, None, nb, 1, D_MODEL), lambda b, t: (layer, j, mrow + b, 0, 0))

    def wspec(shape, resident=False):
        nd = len(shape)
        mode = {"pipeline_mode": pl.Buffered(1)} if resident else {}
        return pl.BlockSpec((None,) + shape, lambda b, t: (layer,) + (0,) * nd, **mode)

    def sspec(shape):
        nd = len(shape)
        return pl.BlockSpec((None, nb) + shape, lambda b, t: (st_layer, b) + (0,) * nd)

    def ospec(shape):
        nd = len(shape)
        return pl.BlockSpec((nb,) + shape, lambda b, t: (b,) + (0,) * nd)

    state_shapes = ((M_HEADS, M_DK, M_DK), (M_HEADS, M_DK), (1, LANES), (CONV_W - 1, S_WIDTH))
    in_specs = [
        pl.BlockSpec((nb, tm, D_MODEL), lambda b, t: (b, t, 0)),
        mod_spec(0), mod_spec(1), mod_spec(2),
        wspec((1, D_MODEL)),
        wspec((D_MODEL, N_Z), True),
        wspec((1, N_Z)),
        wspec((CONV_W, S_WIDTH)),
        wspec((1, D_MODEL)),
        wspec((D_MODEL, D_MODEL), True),
    ] + [sspec(s) for s in state_shapes]
    out_shape = (jax.ShapeDtypeStruct(x.shape, F32),) + tuple(
        jax.ShapeDtypeStruct((bsz,) + s, F32) for s in state_shapes)
    out_specs = (pl.BlockSpec((nb, tm, D_MODEL), lambda b, t: (b, t, 0)),) + tuple(ospec(s) for s in state_shapes)
    nstep = grid[0] * grid[1]
    for a in cast:
        slab = a.shape[1] // nstep
        assert a.shape[1] % nstep == 0 and slab % 16 == 0
        in_specs.append(pl.BlockSpec((None, slab, a.shape[2]), lambda b, t: (layer, b * grid[1] + t, 0)))
        out_shape += (jax.ShapeDtypeStruct(a.shape[1:], BF16),)
        out_specs += (pl.BlockSpec((slab, a.shape[2]), lambda b, t: (b * grid[1] + t, 0)),)
    scratch = [
        pltpu.VMEM((rows, M_WIDTH), BF16),
        pltpu.VMEM((rows, M_WIDTH), BF16),
        pltpu.VMEM((rows, M_WIDTH), BF16),
        pltpu.VMEM((rows, M_WIDTH), F32),
        pltpu.VMEM((rows, M_WIDTH), F32),
        pltpu.VMEM((rows, D_MODEL), BF16),
        pltpu.VMEM((nb, CONV_ROW0 + tm, S_WIDTH), F32),
        pltpu.VMEM((rows, LANES), F32),
        pltpu.VMEM((rows, LANES), F32),
        pltpu.VMEM((rows, LANES), F32),
        pltpu.VMEM((rows, LANES), F32),
        pltpu.VMEM((LANES, rows), F32),
    ]
    kern = functools.partial(_mixer_kernel, nb=nb, tm=tm, seg=seg, nt=t_len // tm, parts=parts, n_cast=len(cast))
    return pl.pallas_call(
        kern, out_shape=out_shape, grid=grid, in_specs=in_specs, out_specs=out_specs,
        scratch_shapes=scratch,
        compiler_params=pltpu.CompilerParams(dimension_semantics=("arbitrary", "arbitrary"),
                                             vmem_limit_bytes=VMEM_LIMIT),
        name=name,
    )(x, mod, mod, mod, w["g1"], w["w_in"], w["b_in"], w["conv_w"], w["g_mix"], w["w_out"],
      c0, n0, m0, cv0, *cast)


def _mlp_tile(x_ref, sh_ref, sc_ref, gt_ref, g2_ref, wup_ref, wdn_ref, gf_ref, xo_ref, *, final):
    nb, tm, _ = x_ref.shape
    rows = nb * tm
    x3 = x_ref[...]
    h3 = _rms_rows(x3) * (g2_ref[...] * (1.0 + sc_ref[...])) + sh_ref[...]
    hb = h3.reshape(rows, D_MODEL).astype(BF16)
    acc = jnp.zeros((rows, D_MODEL), F32)
    for c in range(D_FF // FF_CHUNK):
        lo = c * FF_CHUNK
        up = jnp.dot(hb, wup_ref[:, lo:lo + FF_CHUNK], preferred_element_type=F32)
        act = jnp.square(jnp.maximum(up, 0.0)).astype(BF16)
        acc = acc + jnp.dot(act, wdn_ref[lo:lo + FF_CHUNK, :], preferred_element_type=F32)
    xn = x3 + gt_ref[...] * acc.reshape(nb, tm, D_MODEL)
    if final:
        xn = _rms_rows(xn) * gf_ref[...]
    xo_ref[...] = xn


def _mlp_kernel(xp_ref, shp_ref, scp_ref, gtp_ref, xs_ref, shs_ref, scs_ref, gts_ref, g2_ref, wup_ref, wdn_ref,
                gf_ref, xop_ref, xos_ref, *, n_prompt, final):
    s = pl.program_id(0)
    shared = (g2_ref, wup_ref, wdn_ref, gf_ref)

    @pl.when(s < n_prompt)
    def _():
        _mlp_tile(xp_ref, shp_ref, scp_ref, gtp_ref, *shared, xop_ref, final=final)

    @pl.when(s == n_prompt)
    def _():
        _mlp_tile(xs_ref, shs_ref, scs_ref, gts_ref, *shared, xos_ref, final=final)


def _mlp(xp, xs, mod, mod_row0, layer, w, w_up, w_down, g_final, *, tm, final, name):
    bp, t_len, _ = xp.shape
    bs, ss, _ = xs.shape
    assert t_len % tm == 0
    nt = t_len // tm
    n_prompt = bp * nt

    def ptile(s):
        return jnp.minimum(s, n_prompt - 1)

    def pmod(j):
        return pl.BlockSpec((None, None, 1, 1, D_MODEL), lambda s: (layer, j, mod_row0 + ptile(s) // nt, 0, 0))

    def smod(j):
        return pl.BlockSpec((None, None, bs, 1, D_MODEL), lambda s: (layer, j, 0, 0, 0))

    xp_spec = pl.BlockSpec((1, tm, D_MODEL), lambda s: (ptile(s) // nt, ptile(s) % nt, 0))
    in_specs = [
        xp_spec, pmod(3), pmod(4), pmod(5),
        pl.BlockSpec((bs, ss, D_MODEL), lambda s: (0, 0, 0), pipeline_mode=pl.Buffered(1)), smod(3), smod(4), smod(5),
        pl.BlockSpec((None, 1, D_MODEL), lambda s: (layer, 0, 0)),
        pl.BlockSpec((D_MODEL, D_FF), lambda s: (0, 0), pipeline_mode=pl.Buffered(1)),
        pl.BlockSpec((D_FF, D_MODEL), lambda s: (0, 0), pipeline_mode=pl.Buffered(1)),
        pl.BlockSpec((1, D_MODEL), lambda s: (0, 0)),
    ]
    return pl.pallas_call(
        functools.partial(_mlp_kernel, n_prompt=n_prompt, final=final),
        out_shape=(jax.ShapeDtypeStruct(xp.shape, F32), jax.ShapeDtypeStruct(xs.shape, F32)),
        grid=(n_prompt + 1,), in_specs=in_specs,
        out_specs=(xp_spec, pl.BlockSpec((bs, ss, D_MODEL), lambda s: (0, 0, 0))),
        compiler_params=pltpu.CompilerParams(dimension_semantics=("arbitrary",), vmem_limit_bytes=VMEM_LIMIT),
        name=name,
    )(xp, mod, mod, mod, xs, mod, mod, mod, w["g2"], w_up, w_down, g_final)


def kernel(x_prompt, x_sample, c_prompt, c_sample, state_C, state_n, state_m, state_conv, w_ada, b_ada,
           g_norm1, w_in, b_in, conv_w, g_mix_out, w_out, g_norm2, w_up, w_down, g_final):
    bp, sp, _ = x_prompt.shape
    bs, ss, _ = x_sample.shape

    def relayout_bias(a):
        gi = a[..., REF_OFF_I:REF_OFF_F]
        gf = a[..., REF_OFF_F:REF_OFF_B]
        pad = [(0, 0)] * (a.ndim - 1) + [(0, LANES - M_HEADS)]
        return jnp.concatenate([a[..., :REF_OFF_I], a[..., REF_OFF_B:], jnp.pad(gi, pad), jnp.pad(gf, pad)], axis=-1)

    w = {
        "g1": g_norm1.reshape(DEPTH, 1, D_MODEL),
        "w_in": _relayout_w_in(w_in),
        "b_in": relayout_bias(b_in).reshape(DEPTH, 1, N_Z),
        "conv_w": conv_w,
        "g_mix": g_mix_out.reshape(DEPTH, 1, D_MODEL),
        "w_out": w_out,
        "g2": g_norm2.reshape(DEPTH, 1, D_MODEL),
    }
    gfin = g_final.reshape(1, D_MODEL)

    mod = _modulation(jnp.concatenate([c_sample, c_prompt], axis=0), w_ada, b_ada)

    zero_state = (jnp.zeros((1, bp, M_HEADS, M_DK, M_DK), F32), jnp.zeros((1, bp, M_HEADS, M_DK), F32),
                  jnp.zeros((1, bp, 1, LANES), F32), jnp.zeros((1, bp, CONV_W - 1, S_WIDTH), F32))
    m_pad = jnp.pad(state_m, ((0, 0), (0, 0), (0, LANES - M_HEADS))).reshape(DEPTH, bs, 1, LANES)

    tm_p = min(PROMPT_TILE_ROWS, sp)
    nb_s = min(SAMPLE_SEQS_PER_STEP, bs)
    xp, xs = x_prompt, x_sample
    outs = [[] for _ in range(8)]
    for l in range(DEPTH):
        last = l == DEPTH - 1
        xp, c1, n1, m1, cv1, wu, wd = _mixer(xp, mod, bs, l, w, zero_state + (0,), nb=1, tm=tm_p, parts=2,
                                             name=f"mixer_prompt_{l}", cast=(w_up, w_down))
        xs, c2, n2, m2, cv2 = _mixer(xs, mod, 0, l, w, (state_C, state_n, m_pad, state_conv, l),
                                     nb=nb_s, tm=ss, parts=1, name=f"mixer_sample_{l}")
        for lst, val in zip(outs, (c1, n1, m1[:, 0, :M_HEADS], cv1, c2, n2, m2[:, 0, :M_HEADS], cv2)):
            lst.append(val)
        xp, xs = _mlp(xp, xs, mod, bs, l, w, wu, wd, gfin, tm=tm_p, final=last, name=f"mlp_{l}")
    return (xp, xs) + tuple(jnp.stack(o) for o in outs)
```

```python
import functools
import types

import jax
import jax.numpy as jnp
from jax import lax
from jax.experimental import pallas as pl
from jax.experimental.pallas import tpu as pltpu

F32 = jnp.float32
BF16 = jnp.bfloat16

D_MODEL = 1024
DEPTH = 4
M_HEADS = 4
M_DK = 128
M_WIDTH = M_HEADS * M_DK
S_WIDTH = 512
S_GROUPS = 4
CONV_W = 3
D_FF = 4 * D_MODEL
N_MOD = 6
EPS = 1e-6

REF_OFF_I = 4 * M_WIDTH
REF_OFF_F = REF_OFF_I + M_HEADS
REF_OFF_B = REF_OFF_F + M_HEADS
REF_N_IN = REF_OFF_B + 3 * S_WIDTH

LANES = 128
SUBLANES = 8
BLK = 128
CONV_ROW0 = SUBLANES
OFF_Q, OFF_K, OFF_V, OFF_O = 0, 512, 1024, 1536
OFF_B, OFF_C, OFF_X = 2048, 2560, 3072
OFF_GI, OFF_GF = 3584, 3712
N_Z = 3840
FF_CHUNK = 1024
RELAYOUT_COLS = 256
VMEM_LIMIT = 58 * 1024 * 1024
PROMPT_TILE_ROWS = 1024
PROMPT_PARTS = 4
SAMPLE_SEQS_PER_STEP = 16


def _rms_rows(x):
    return x * lax.rsqrt(jnp.mean(x * x, axis=-1, keepdims=True) + EPS)


def _log_sigmoid(x):
    return jnp.minimum(x, 0.0) - jnp.log1p(jnp.exp(-jnp.abs(x)))


MOD_PER_STEP = 2


def _mod_kernel(c_ref, w_ref, b_ref, o_ref):
    c = c_ref[...]
    a = (c * jax.nn.sigmoid(c)).astype(BF16)
    res = jnp.dot(a, w_ref[...].astype(BF16), preferred_element_type=F32)
    for k in range(MOD_PER_STEP):
        o_ref[k] = res[:, k * D_MODEL:(k + 1) * D_MODEL] + b_ref[k]


def _modulation(c_all, w_ada, b_ada):
    nrow = c_all.shape[0]
    b4 = b_ada.reshape(DEPTH, N_MOD, 1, D_MODEL)
    out = pl.pallas_call(
        _mod_kernel,
        out_shape=jax.ShapeDtypeStruct((DEPTH, N_MOD, nrow, D_MODEL), F32),
        grid=(DEPTH, N_MOD // MOD_PER_STEP),
        in_specs=[
            pl.BlockSpec((nrow, D_MODEL), lambda l, j: (0, 0)),
            pl.BlockSpec((None, D_MODEL, MOD_PER_STEP * D_MODEL), lambda l, j: (l, 0, j)),
            pl.BlockSpec((None, MOD_PER_STEP, 1, D_MODEL), lambda l, j: (l, j, 0, 0)),
        ],
        out_specs=pl.BlockSpec((None, MOD_PER_STEP, nrow, D_MODEL), lambda l, j: (l, j, 0, 0)),
        compiler_params=pltpu.CompilerParams(dimension_semantics=("arbitrary", "arbitrary"),
                                             vmem_limit_bytes=VMEM_LIMIT),
        name="adaln_modulation",
    )(c_all, w_ada, b4)
    return out.reshape(DEPTH, N_MOD, nrow, 1, D_MODEL)


def _relayout_kernel(wt_ref, o_ref):
    def put(col0, row0, ncol):
        for c in range(0, ncol, RELAYOUT_COLS):
            o_ref[:, col0 + c:col0 + c + RELAYOUT_COLS] = (
                wt_ref[row0 + c:row0 + c + RELAYOUT_COLS, :].T.astype(BF16))

    put(0, 0, REF_OFF_I)
    put(OFF_B, REF_OFF_B, REF_N_IN - REF_OFF_B)
    g = wt_ref[REF_OFF_I:REF_OFF_I + LANES, :].T
    lane = lax.broadcasted_iota(jnp.int32, g.shape, 1)
    o_ref[:, OFF_GI:OFF_GF] = jnp.where(lane < M_HEADS, g, 0.0).astype(BF16)
    o_ref[:, OFF_GF:N_Z] = jnp.where(lane < M_HEADS, pltpu.roll(g, LANES - M_HEADS, axis=1), 0.0).astype(BF16)


def _relayout_w_in(w_in):
    wt = jnp.swapaxes(w_in, 1, 2)
    return pl.pallas_call(
        _relayout_kernel,
        out_shape=jax.ShapeDtypeStruct((DEPTH, D_MODEL, N_Z), BF16),
        grid=(DEPTH,),
        in_specs=[pl.BlockSpec((None, REF_N_IN, D_MODEL), lambda l: (l, 0, 0))],
        out_specs=pl.BlockSpec((None, D_MODEL, N_Z), lambda l: (l, 0, 0)),
        compiler_params=pltpu.CompilerParams(dimension_semantics=("arbitrary",), vmem_limit_bytes=VMEM_LIMIT),
        name="relayout_w_in",
    )(wt)


def _mixer_part(r, *, nb, tm, seg, first_tile, lo, th):
    assert th == tm or nb == 1
    rows = nb * th
    fsl = slice(lo, lo + rows)
    tsl = slice(lo, lo + th)
    nblk = rows // BLK
    blk0 = lo // BLK
    nseg = BLK // seg
    v = types.SimpleNamespace(blk={})

    if first_tile is None:
        assert lo == 0
        c_in, n_in, m_in, cv_in = r.c0, r.n0, r.m0, r.cv0
    else:
        c_in, n_in, m_in, cv_in = r.c, r.n, r.m, r.cv

    def proj(off, width):
        return (jnp.dot(v.hb, r.win[:, off:off + width], preferred_element_type=F32)
                + r.bin[:, off:off + width])

    def norm_gates():
        if first_tile is not None and lo == 0:
            @pl.when(first_tile)
            def _():
                r.c[...] = r.c0[...]
                r.n[...] = r.n0[...]
                r.m[...] = r.m0[...]
                r.cv[...] = r.cv0[...]

        v.x3 = r.x[:, tsl, :]
        h3 = _rms_rows(v.x3) * (r.g1[...] * (1.0 + r.sc[...])) + r.sh[...]
        v.hb = h3.reshape(rows, D_MODEL).astype(BF16)
        v.gates = proj(OFF_GI, 2 * LANES)

    def q_proj():
        r.q_s[fsl, :] = proj(OFF_Q, M_WIDTH).astype(BF16)

    def gate_scalars():
        logi = v.gates[:, :LANES]
        logf = _log_sigmoid(v.gates[:, LANES:])
        rin = lax.broadcasted_iota(jnp.int32, (rows, LANES), 0) & (th - 1)
        bt = logf
        shift = 1
        while shift < th:
            bt = bt + jnp.where(rin >= shift, pltpu.roll(bt, shift, axis=0), 0.0)
            shift *= 2
        rr = logi - bt
        cm = rr
        shift = 1
        while shift < th:
            cm = jnp.maximum(cm, jnp.where(rin >= shift, pltpu.roll(cm, shift, axis=0), -jnp.inf))
            shift *= 2
        m0b = jnp.broadcast_to(m_in[...], (nb, th, LANES)).reshape(rows, LANES)
        cmx = jnp.maximum(m0b, cm)
        p = -cmx
        m = bt + cmx
        r.m[...] = m.reshape(nb, th, LANES)[:, th - 1:th, :]
        nchunk = rows // seg
        p3 = p.reshape(nchunk, seg, LANES)
        p_before = jnp.where(rin == 0, -m0b, pltpu.roll(p, 1, axis=0)).reshape(nchunk, seg, LANES)[:, 0:1, :]
        r.p_s[fsl, :] = p
        r.g_s[fsl, :] = jnp.exp(p3 - p_before).reshape(rows, LANES)
        r.we_s[fsl, :] = jnp.exp(p3[:, seg - 1:seg, :] + rr.reshape(nchunk, seg, LANES)).reshape(rows, LANES)
        r.em_s[fsl, :] = jnp.exp(-m)
        for bi in range(nblk):
            r.rt_s[:, lo + bi * BLK:lo + (bi + 1) * BLK] = rr[bi * BLK:(bi + 1) * BLK, :].T

    def k_proj():
        r.k_s[fsl, :] = (proj(OFF_K, M_WIDTH) * (M_DK ** -0.5)).astype(BF16)

    def v_proj():
        r.v_s[fsl, :] = proj(OFF_V, M_WIDTH).astype(BF16)

    def o_proj():
        r.og_s[fsl, :] = jax.nn.sigmoid(proj(OFF_O, M_WIDTH))

    def block_diag(a, rowseg):
        if nseg == 1:
            return a
        zero = jnp.zeros_like(a)
        return jnp.concatenate([jnp.where(rowseg == j, a, zero) for j in range(nseg)], axis=1)

    def block_consts(bi):
        ri = lax.broadcasted_iota(jnp.int32, (BLK, BLK), 0)
        ci = lax.broadcasted_iota(jnp.int32, (BLK, BLK), 1)
        sh = seg.bit_length() - 1
        rowseg = ri >> sh
        mask = ci <= ri
        if nseg > 1:
            mask = mask & (rowseg == (ci >> sh))
        r0 = bi * BLK
        rsl = slice(r0, r0 + BLK)
        seq0 = r0 // tm
        return rowseg, mask, rsl, slice(seq0, seq0 + nseg)

    def block_scores(bi):
        rowseg, mask, rsl, _ = block_consts(bi)
        hsl = [slice(h * M_DK, (h + 1) * M_DK) for h in range(M_HEADS)]
        q = [r.q_s[rsl, hs] for hs in hsl]
        k = [r.k_s[rsl, hs] for hs in hsl]
        vv = [r.v_s[rsl, hs] for hs in hsl]
        s = [lax.dot_general(q[h], k[h], (((1,), (1,)), ((), ())), preferred_element_type=F32)
             for h in range(M_HEADS)]
        sw = [s[h] * jnp.exp(jnp.where(mask, r.p_s[rsl, h:h + 1] + r.rt_s[h:h + 1, rsl], -jnp.inf))
              for h in range(M_HEADS)]
        kw = [k[h].astype(F32) * r.we_s[rsl, h:h + 1] for h in range(M_HEADS)]
        blk = types.SimpleNamespace(q=q)
        blk.rowsum = [jnp.sum(sw[h], axis=-1, keepdims=True) for h in range(M_HEADS)]
        blk.intra = [jnp.dot(sw[h].astype(BF16), vv[h], preferred_element_type=F32) for h in range(M_HEADS)]
        blk.upd = [lax.dot_general(block_diag(kw[h].astype(BF16), rowseg), vv[h], (((0,), (0,)), ((), ())),
                                   preferred_element_type=F32) for h in range(M_HEADS)]
        blk.ksum = [jnp.sum(kw[h].reshape(nseg, seg, M_DK), axis=1, keepdims=True) for h in range(M_HEADS)]
        v.blk[bi] = blk

    def block_state(bi):
        rowseg, _, rsl, ssl = block_consts(bi)
        blk = v.blk.pop(bi)
        g_blk = r.g_s[rsl, :].reshape(nseg, seg, LANES)
        heads = range(M_HEADS)
        c0 = [c_in[ssl, h] for h in heads]
        n0 = [n_in[ssl, h:h + 1, :] for h in heads]
        inter = [jnp.dot(block_diag(blk.q[h], rowseg), c0[h].astype(BF16).reshape(nseg * M_DK, M_DK),
                         preferred_element_type=F32) for h in heads]
        qn = [jnp.sum(blk.q[h].astype(F32) * jnp.broadcast_to(n0[h], (nseg, seg, M_DK)).reshape(BLK, M_DK),
                      axis=-1, keepdims=True) for h in heads]
        gc = [r.g_s[rsl, h:h + 1] for h in heads]
        num = [gc[h] * inter[h] + blk.intra[h] for h in heads]
        den = [gc[h] * qn[h] + blk.rowsum[h] for h in heads]
        hh = [num[h] * (1.0 / jnp.maximum(jnp.abs(den[h]), r.em_s[rsl, h:h + 1])) for h in heads]
        for h in heads:
            r.hn_s[rsl, h * M_DK:(h + 1) * M_DK] = _rms_rows(hh[h])
        for h in heads:
            decay = g_blk[:, seg - 1:seg, h:h + 1]
            r.c[ssl, h] = decay * c0[h] + blk.upd[h].reshape(nseg, M_DK, M_DK)
            r.n[ssl, h:h + 1, :] = decay * n0[h] + blk.ksum[h]

    def conv_inputs():
        if lo == 0:
            r.ubuf_s[:, CONV_ROW0 - CONV_W + 1:CONV_ROW0, :] = cv_in[...]
        u = proj(OFF_C, S_WIDTH) * proj(OFF_X, S_WIDTH)
        r.ubuf_s[:, CONV_ROW0 + lo:CONV_ROW0 + lo + th, :] = u.reshape(nb, th, S_WIDTH)
        r.cv[...] = r.ubuf_s[:, CONV_ROW0 + lo + th - CONV_W + 1:CONV_ROW0 + lo + th, :]

    def conv_outputs():
        bg = proj(OFF_B, S_WIDTH)
        cw = r.cw[...]
        first = CONV_ROW0 + lo - CONV_W + 1
        yc = cw[0:1] * r.ubuf_s[:, first:first + th, :]
        for j in range(1, CONV_W):
            yc = yc + cw[j:j + 1] * r.ubuf_s[:, first + j:first + j + th, :]
        ysv = (bg.reshape(nb, th, S_WIDTH) * yc).reshape(rows, S_WIDTH)
        for grp in range(S_GROUPS):
            cs = slice(M_WIDTH + grp * LANES, M_WIDTH + (grp + 1) * LANES)
            r.mix_s[fsl, cs] = (_rms_rows(ysv[:, grp * LANES:(grp + 1) * LANES]) * r.gmix[:, cs]).astype(BF16)

    def conv_half_out():
        v.y_conv = jnp.dot(r.mix_s[fsl, M_WIDTH:], r.wout[M_WIDTH:, :], preferred_element_type=F32)

    def finish():
        r.mix_s[fsl, :M_WIDTH] = (r.hn_s[fsl, :] * r.og_s[fsl, :] * r.gmix[:, :M_WIDTH]).astype(BF16)
        y = v.y_conv + jnp.dot(r.mix_s[fsl, :M_WIDTH], r.wout[:M_WIDTH, :], preferred_element_type=F32)
        r.xo[:, tsl, :] = v.x3 + r.gt[...] * y.reshape(nb, th, D_MODEL)

    blocks = [functools.partial(block_scores, blk0)]
    for bi in range(blk0, blk0 + nblk):
        if bi + 1 < blk0 + nblk:
            blocks.append(functools.partial(block_scores, bi + 1))
        blocks.append(functools.partial(block_state, bi))
    return types.SimpleNamespace(
        pre=[norm_gates, q_proj, gate_scalars, k_proj, v_proj], blocks=blocks,
        fillers=[o_proj, conv_inputs, conv_outputs, conv_half_out], finish=finish)


def _spread(chain, others):
    out, done = [], 0
    for i, step in enumerate(chain):
        out.append(step)
        want = ((i + 1) * len(others)) // len(chain)
        out.extend(others[done:want])
        done = want
    return out


def _mixer_steps(r, *, nb, tm, seg, first_tile, parts):
    th = tm // parts
    part = [_mixer_part(r, nb=nb, tm=tm, seg=seg, first_tile=first_tile, lo=i * th, th=th) for i in range(parts)]
    order = list(part[0].pre)
    for i, p in enumerate(part):
        others = list(part[i + 1].pre) if i + 1 < parts else []
        if i > 0:
            others += part[i - 1].fillers[2:] + [part[i - 1].finish]
        others += p.fillers[:2]
        order += _spread(p.blocks, others)
    return order + part[-1].fillers[2:] + [part[-1].finish]


_MIXER_REFS = (
    "x", "sh", "sc", "gt", "g1", "win", "bin", "cw", "gmix", "wout", "c0", "n0", "m0", "cv0",
    "xo", "c", "n", "m", "cv",
    "q_s", "k_s", "v_s", "og_s", "hn_s", "mix_s", "ubuf_s", "p_s", "g_s", "we_s", "em_s", "rt_s",
)


def _mixer_kernel(*refs, nb, tm, seg, nt, parts, n_cast):
    n_in, n_out = _MIXER_REFS.index("xo"), _MIXER_REFS.index("q_s")
    cast_in = refs[n_in:n_in + n_cast]
    cast_out = refs[n_out + n_cast:n_out + 2 * n_cast]
    refs = refs[:n_in] + refs[n_in + n_cast:n_out + n_cast] + refs[n_out + 2 * n_cast:]
    r = types.SimpleNamespace(**dict(zip(_MIXER_REFS, refs, strict=True)))
    first_tile = None if nt == 1 else pl.program_id(1) == 0
    for step in _mixer_steps(r, nb=nb, tm=tm, seg=seg, first_tile=first_tile, parts=parts):
        step()
    for src_ref, dst_ref in zip(cast_in, cast_out):
        dst_ref[...] = src_ref[...].astype(BF16)


def _mixer(x, mod, mod_row0, layer, w, state, *, nb, tm, parts, name, cast=()):
    bsz, t_len, _ = x.shape
    rows = nb * tm
    seg = min(BLK, tm // parts)
    assert (rows // parts) % BLK == 0 and bsz % nb == 0 and t_len % tm == 0 and mod_row0 % nb == 0
    assert tm & (tm - 1) == 0 and (nb == 1 or tm == t_len)
    c0, n0, m0, cv0, st_layer = state
    grid = (bsz // nb, t_len // tm)
    mrow = mod_row0 // nb

    def mod_spec(j):
        return pl.BlockSpec((None, None, nb, 1, D_MODEL), lambda b, t: (layer, j, mrow + b, 0, 0))

    def wspec(shape, resident=False):
        nd = len(shape)
        mode = {"pipeline_mode": pl.Buffered(1)} if resident else {}
        return pl.BlockSpec((None,) + shape, lambda b, t: (layer,) + (0,) * nd, **mode)

    def sspec(shape):
        nd = len(shape)
        return pl.BlockSpec((None, nb) + shape, lambda b, t: (st_layer, b) + (0,) * nd)

    def ospec(shape):
        nd = len(shape)
        return pl.BlockSpec((nb,) + shape, lambda b, t: (b,) + (0,) * nd)

    state_shapes = ((M_HEADS, M_DK, M_DK), (M_HEADS, M_DK), (1, LANES), (CONV_W - 1, S_WIDTH))
    in_specs = [
        pl.BlockSpec((nb, tm, D_MODEL), lambda b, t: (b, t, 0)),
        mod_spec(0), mod_spec(1), mod_spec(2),
        wspec((1, D_MODEL)),
        wspec((D_MODEL, N_Z), True),
        wspec((1, N_Z)),
        wspec((CONV_W, S_WIDTH)),
        wspec((1, D_MODEL)),
        wspec((D_MODEL, D_MODEL), True),
    ] + [sspec(s) for s in state_shapes]
    out_shape = (jax.ShapeDtypeStruct(x.shape, F32),) + tuple(
        jax.ShapeDtypeStruct((bsz,) + s, F32) for s in state_shapes)
    out_specs = (pl.BlockSpec((nb, tm, D_MODEL), lambda b, t: (b, t, 0)),) + tuple(ospec(s) for s in state_shapes)
    nstep = grid[0] * grid[1]
    for a in cast:
        slab = a.shape[1] // nstep
        assert a.shape[1] % nstep == 0 and slab % 16 == 0
        in_specs.append(pl.BlockSpec((None, slab, a.shape[2]), lambda b, t: (layer, b * grid[1] + t, 0)))
        out_shape += (jax.ShapeDtypeStruct(a.shape[1:], BF16),)
        out_specs += (pl.BlockSpec((slab, a.shape[2]), lambda b, t: (b * grid[1] + t, 0)),)
    scratch = [
        pltpu.VMEM((rows, M_WIDTH), BF16),
        pltpu.VMEM((rows, M_WIDTH), BF16),
        pltpu.VMEM((rows, M_WIDTH), BF16),
        pltpu.VMEM((rows, M_WIDTH), F32),
        pltpu.VMEM((rows, M_WIDTH), F32),
        pltpu.VMEM((rows, D_MODEL), BF16),
        pltpu.VMEM((nb, CONV_ROW0 + tm, S_WIDTH), F32),
        pltpu.VMEM((rows, LANES), F32),
        pltpu.VMEM((rows, LANES), F32),
        pltpu.VMEM((rows, LANES), F32),
        pltpu.VMEM((rows, LANES), F32),
        pltpu.VMEM((LANES, rows), F32),
    ]
    kern = functools.partial(_mixer_kernel, nb=nb, tm=tm, seg=seg, nt=t_len // tm, parts=parts, n_cast=len(cast))
    return pl.pallas_call(
        kern, out_shape=out_shape, grid=grid, in_specs=in_specs, out_specs=out_specs,
        scratch_shapes=scratch,
        compiler_params=pltpu.CompilerParams(dimension_semantics=("arbitrary", "arbitrary"),
                                             vmem_limit_bytes=VMEM_LIMIT),
        name=name,
    )(x, mod, mod, mod, w["g1"], w["w_in"], w["b_in"], w["conv_w"], w["g_mix"], w["w_out"],
      c0, n0, m0, cv0, *cast)


def _mlp_tile(x_ref, sh_ref, sc_ref, gt_ref, g2_ref, wup_ref, wdn_ref, gf_ref, xo_ref, *, final):
    nb, tm, _ = x_ref.shape
    rows = nb * tm
    x3 = x_ref[...]
    h3 = _rms_rows(x3) * (g2_ref[...] * (1.0 + sc_ref[...])) + sh_ref[...]
    hb = h3.reshape(rows, D_MODEL).astype(BF16)
    acc = jnp.zeros((rows, D_MODEL), F32)
    for c in range(D_FF // FF_CHUNK):
        lo = c * FF_CHUNK
        up = jnp.dot(hb, wup_ref[:, lo:lo + FF_CHUNK], preferred_element_type=F32)
        act = jnp.square(jnp.maximum(up, 0.0)).astype(BF16)
        acc = acc + jnp.dot(act, wdn_ref[lo:lo + FF_CHUNK, :], preferred_element_type=F32)
    xn = x3 + gt_ref[...] * acc.reshape(nb, tm, D_MODEL)
    if final:
        xn = _rms_rows(xn) * gf_ref[...]
    xo_ref[...] = xn


def _mlp_kernel(xp_ref, shp_ref, scp_ref, gtp_ref, xs_ref, shs_ref, scs_ref, gts_ref, g2_ref, wup_ref, wdn_ref,
                gf_ref, xop_ref, xos_ref, *, n_prompt, final):
    s = pl.program_id(0)
    shared = (g2_ref, wup_ref, wdn_ref, gf_ref)

    @pl.when(s < n_prompt)
    def _():
        _mlp_tile(xp_ref, shp_ref, scp_ref, gtp_ref, *shared, xop_ref, final=final)

    @pl.when(s == n_prompt)
    def _():
        _mlp_tile(xs_ref, shs_ref, scs_ref, gts_ref, *shared, xos_ref, final=final)


def _mlp(xp, xs, mod, mod_row0, layer, w, w_up, w_down, g_final, *, tm, final, name):
    bp, t_len, _ = xp.shape
    bs, ss, _ = xs.shape
    assert t_len % tm == 0
    nt = t_len // tm
    n_prompt = bp * nt

    def ptile(s):
        return jnp.minimum(s, n_prompt - 1)

    def pmod(j):
        return pl.BlockSpec((None, None, 1, 1, D_MODEL), lambda s: (layer, j, mod_row0 + ptile(s) // nt, 0, 0))

    def smod(j):
        return pl.BlockSpec((None, None, bs, 1, D_MODEL), lambda s: (layer, j, 0, 0, 0))

    xp_spec = pl.BlockSpec((1, tm, D_MODEL), lambda s: (ptile(s) // nt, ptile(s) % nt, 0))
    in_specs = [
        xp_spec, pmod(3), pmod(4), pmod(5),
        pl.BlockSpec((bs, ss, D_MODEL), lambda s: (0, 0, 0), pipeline_mode=pl.Buffered(1)), smod(3), smod(4), smod(5),
        pl.BlockSpec((None, 1, D_MODEL), lambda s: (layer, 0, 0)),
        pl.BlockSpec((D_MODEL, D_FF), lambda s: (0, 0), pipeline_mode=pl.Buffered(1)),
        pl.BlockSpec((D_FF, D_MODEL), lambda s: (0, 0), pipeline_mode=pl.Buffered(1)),
        pl.BlockSpec((1, D_MODEL), lambda s: (0, 0)),
    ]
    return pl.pallas_call(
        functools.partial(_mlp_kernel, n_prompt=n_prompt, final=final),
        out_shape=(jax.ShapeDtypeStruct(xp.shape, F32), jax.ShapeDtypeStruct(xs.shape, F32)),
        grid=(n_prompt + 1,), in_specs=in_specs,
        out_specs=(xp_spec, pl.BlockSpec((bs, ss, D_MODEL), lambda s: (0, 0, 0))),
        compiler_params=pltpu.CompilerParams(dimension_semantics=("arbitrary",), vmem_limit_bytes=VMEM_LIMIT),
        name=name,
    )(xp, mod, mod, mod, xs, mod, mod, mod, w["g2"], w_up, w_down, g_final)


def kernel(x_prompt, x_sample, c_prompt, c_sample, state_C, state_n, state_m, state_conv, w_ada, b_ada,
           g_norm1, w_in, b_in, conv_w, g_mix_out, w_out, g_norm2, w_up, w_down, g_final):
    bp, sp, _ = x_prompt.shape
    bs, ss, _ = x_sample.shape

    def relayout_bias(a):
        gi = a[..., REF_OFF_I:REF_OFF_F]
        gf = a[..., REF_OFF_F:REF_OFF_B]
        pad = [(0, 0)] * (a.ndim - 1) + [(0, LANES - M_HEADS)]
        return jnp.concatenate([a[..., :REF_OFF_I], a[..., REF_OFF_B:], jnp.pad(gi, pad), jnp.pad(gf, pad)], axis=-1)

    w = {
        "g1": g_norm1.reshape(DEPTH, 1, D_MODEL),
        "w_in": _relayout_w_in(w_in),
        "b_in": relayout_bias(b_in).reshape(DEPTH, 1, N_Z),
        "conv_w": conv_w,
        "g_mix": g_mix_out.reshape(DEPTH, 1, D_MODEL),
        "w_out": w_out,
        "g2": g_norm2.reshape(DEPTH, 1, D_MODEL),
    }
    gfin = g_final.reshape(1, D_MODEL)

    mod = _modulation(jnp.concatenate([c_sample, c_prompt], axis=0), w_ada, b_ada)

    zero_state = (jnp.zeros((1, bp, M_HEADS, M_DK, M_DK), F32), jnp.zeros((1, bp, M_HEADS, M_DK), F32),
                  jnp.zeros((1, bp, 1, LANES), F32), jnp.zeros((1, bp, CONV_W - 1, S_WIDTH), F32))
    m_pad = jnp.pad(state_m, ((0, 0), (0, 0), (0, LANES - M_HEADS))).reshape(DEPTH, bs, 1, LANES)

    tm_p = min(PROMPT_TILE_ROWS, sp)
    nb_s = min(SAMPLE_SEQS_PER_STEP, bs)
    xp, xs = x_prompt, x_sample
    outs = [[] for _ in range(8)]
    for l in range(DEPTH):
        last = l == DEPTH - 1
        xp, c1, n1, m1, cv1, wu, wd = _mixer(xp, mod, bs, l, w, zero_state + (0,), nb=1, tm=tm_p, parts=PROMPT_PARTS,
                                             name=f"mixer_prompt_{l}", cast=(w_up, w_down))
        xs, c2, n2, m2, cv2 = _mixer(xs, mod, 0, l, w, (state_C, state_n, m_pad, state_conv, l),
                                     nb=nb_s, tm=ss, parts=1, name=f"mixer_sample_{l}")
        for lst, val in zip(outs, (c1, n1, m1[:, 0, :M_HEADS], cv1, c2, n2, m2[:, 0, :M_HEADS], cv2)):
            lst.append(val)
        xp, xs = _mlp(xp, xs, mod, bs, l, w, wu, wd, gfin, tm=tm_p, final=last, name=f"mlp_{l}")
    return (xp, xs) + tuple(jnp.stack(o) for o in outs)
```

```python
import functools
import types

import jax
import jax.numpy as jnp
from jax import lax
from jax.experimental import pallas as pl
from jax.experimental.pallas import tpu as pltpu

F32 = jnp.float32
BF16 = jnp.bfloat16

D_MODEL = 1024
DEPTH = 4
M_HEADS = 4
M_DK = 128
M_WIDTH = M_HEADS * M_DK
S_WIDTH = 512
S_GROUPS = 4
CONV_W = 3
D_FF = 4 * D_MODEL
N_MOD = 6
EPS = 1e-6

REF_OFF_I = 4 * M_WIDTH
REF_OFF_F = REF_OFF_I + M_HEADS
REF_OFF_B = REF_OFF_F + M_HEADS
REF_N_IN = REF_OFF_B + 3 * S_WIDTH

LANES = 128
SUBLANES = 8
BLK = 128
CONV_ROW0 = SUBLANES
OFF_Q, OFF_K, OFF_V, OFF_O = 0, 512, 1024, 1536
OFF_B, OFF_C, OFF_X = 2048, 2560, 3072
OFF_GI, OFF_GF = 3584, 3712
N_Z = 3840
FF_CHUNK = 1024
RELAYOUT_COLS = 256
VMEM_LIMIT = 58 * 1024 * 1024
PROMPT_TILE_ROWS = 1024
PROMPT_PARTS = 4
SAMPLE_SEQS_PER_STEP = 16


def _rms_rows(x):
    return x * lax.rsqrt(jnp.mean(x * x, axis=-1, keepdims=True) + EPS)


def _log_sigmoid(x):
    return jnp.minimum(x, 0.0) - jnp.log1p(jnp.exp(-jnp.abs(x)))


MOD_PER_STEP = 2


def _mod_kernel(c_ref, w_ref, b_ref, o_ref):
    c = c_ref[...]
    a = (c * jax.nn.sigmoid(c)).astype(BF16)
    res = jnp.dot(a, w_ref[...].astype(BF16), preferred_element_type=F32)
    for k in range(MOD_PER_STEP):
        o_ref[k] = res[:, k * D_MODEL:(k + 1) * D_MODEL] + b_ref[k]


def _modulation(c_all, w_ada, b_ada):
    nrow = c_all.shape[0]
    b4 = b_ada.reshape(DEPTH, N_MOD, 1, D_MODEL)
    out = pl.pallas_call(
        _mod_kernel,
        out_shape=jax.ShapeDtypeStruct((DEPTH, N_MOD, nrow, D_MODEL), F32),
        grid=(DEPTH, N_MOD // MOD_PER_STEP),
        in_specs=[
            pl.BlockSpec((nrow, D_MODEL), lambda l, j: (0, 0)),
            pl.BlockSpec((None, D_MODEL, MOD_PER_STEP * D_MODEL), lambda l, j: (l, 0, j)),
            pl.BlockSpec((None, MOD_PER_STEP, 1, D_MODEL), lambda l, j: (l, j, 0, 0)),
        ],
        out_specs=pl.BlockSpec((None, MOD_PER_STEP, nrow, D_MODEL), lambda l, j: (l, j, 0, 0)),
        compiler_params=pltpu.CompilerParams(dimension_semantics=("arbitrary", "arbitrary"),
                                             vmem_limit_bytes=VMEM_LIMIT),
        name="adaln_modulation",
    )(c_all, w_ada, b4)
    return out.reshape(DEPTH, N_MOD, nrow, 1, D_MODEL)


def _relayout_kernel(wt_ref, o_ref):
    def put(col0, row0, ncol):
        for c in range(0, ncol, RELAYOUT_COLS):
            o_ref[:, col0 + c:col0 + c + RELAYOUT_COLS] = (
                wt_ref[row0 + c:row0 + c + RELAYOUT_COLS, :].T.astype(BF16))

    put(0, 0, REF_OFF_I)
    put(OFF_B, REF_OFF_B, REF_N_IN - REF_OFF_B)
    g = wt_ref[REF_OFF_I:REF_OFF_I + LANES, :].T
    lane = lax.broadcasted_iota(jnp.int32, g.shape, 1)
    o_ref[:, OFF_GI:OFF_GF] = jnp.where(lane < M_HEADS, g, 0.0).astype(BF16)
    o_ref[:, OFF_GF:N_Z] = jnp.where(lane < M_HEADS, pltpu.roll(g, LANES - M_HEADS, axis=1), 0.0).astype(BF16)


def _relayout_w_in(w_in):
    wt = jnp.swapaxes(w_in, 1, 2)
    return pl.pallas_call(
        _relayout_kernel,
        out_shape=jax.ShapeDtypeStruct((DEPTH, D_MODEL, N_Z), BF16),
        grid=(DEPTH,),
        in_specs=[pl.BlockSpec((None, REF_N_IN, D_MODEL), lambda l: (l, 0, 0))],
        out_specs=pl.BlockSpec((None, D_MODEL, N_Z), lambda l: (l, 0, 0)),
        compiler_params=pltpu.CompilerParams(dimension_semantics=("arbitrary",), vmem_limit_bytes=VMEM_LIMIT),
        name="relayout_w_in",
    )(wt)


def _mixer_part(r, *, nb, tm, seg, first_tile, lo, th):
    assert th == tm or nb == 1
    rows = nb * th
    fsl = slice(lo, lo + rows)
    tsl = slice(lo, lo + th)
    nblk = rows // BLK
    blk0 = lo // BLK
    nseg = BLK // seg
    v = types.SimpleNamespace(blk={})

    if first_tile is None:
        assert lo == 0
        c_in, n_in, m_in, cv_in = r.c0, r.n0, r.m0, r.cv0
    else:
        c_in, n_in, m_in, cv_in = r.c, r.n, r.m, r.cv

    def proj(off, width):
        return (jnp.dot(v.hb, r.win[:, off:off + width], preferred_element_type=F32)
                + r.bin[:, off:off + width])

    def norm_gates():
        if first_tile is not None and lo == 0:
            @pl.when(first_tile)
            def _():
                r.c[...] = r.c0[...]
                r.n[...] = r.n0[...]
                r.m[...] = r.m0[...]
                r.cv[...] = r.cv0[...]

        v.x3 = r.x[:, tsl, :]
        h3 = _rms_rows(v.x3) * (r.g1[...] * (1.0 + r.sc[...])) + r.sh[...]
        v.hb = h3.reshape(rows, D_MODEL).astype(BF16)
        v.gates = proj(OFF_GI, 2 * LANES)

    def q_proj():
        r.q_s[fsl, :] = proj(OFF_Q, M_WIDTH).astype(BF16)

    def gate_scalars():
        logi = v.gates[:, :LANES]
        logf = _log_sigmoid(v.gates[:, LANES:])
        rin = lax.broadcasted_iota(jnp.int32, (rows, LANES), 0) & (th - 1)
        bt = logf
        shift = 1
        while shift < th:
            bt = bt + jnp.where(rin >= shift, pltpu.roll(bt, shift, axis=0), 0.0)
            shift *= 2
        rr = logi - bt
        cm = rr
        shift = 1
        while shift < th:
            cm = jnp.maximum(cm, jnp.where(rin >= shift, pltpu.roll(cm, shift, axis=0), -jnp.inf))
            shift *= 2
        m0b = jnp.broadcast_to(m_in[...], (nb, th, LANES)).reshape(rows, LANES)
        cmx = jnp.maximum(m0b, cm)
        p = -cmx
        m = bt + cmx
        r.m[...] = m.reshape(nb, th, LANES)[:, th - 1:th, :]
        nchunk = rows // seg
        p3 = p.reshape(nchunk, seg, LANES)
        p_before = jnp.where(rin == 0, -m0b, pltpu.roll(p, 1, axis=0)).reshape(nchunk, seg, LANES)[:, 0:1, :]
        r.p_s[fsl, :] = p
        r.g_s[fsl, :] = jnp.exp(p3 - p_before).reshape(rows, LANES)
        r.we_s[fsl, :] = jnp.exp(p3[:, seg - 1:seg, :] + rr.reshape(nchunk, seg, LANES)).reshape(rows, LANES)
        r.em_s[fsl, :] = jnp.exp(-m)
        for bi in range(nblk):
            r.rt_s[:, lo + bi * BLK:lo + (bi + 1) * BLK] = rr[bi * BLK:(bi + 1) * BLK, :].T

    def k_proj():
        r.k_s[fsl, :] = (proj(OFF_K, M_WIDTH) * (M_DK ** -0.5)).astype(BF16)

    def v_proj():
        r.v_s[fsl, :] = proj(OFF_V, M_WIDTH).astype(BF16)

    def o_proj():
        r.og_s[fsl, :] = jax.nn.sigmoid(proj(OFF_O, M_WIDTH))

    def block_diag(a, rowseg):
        if nseg == 1:
            return a
        zero = jnp.zeros_like(a)
        return jnp.concatenate([jnp.where(rowseg == j, a, zero) for j in range(nseg)], axis=1)

    def block_consts(bi):
        ri = lax.broadcasted_iota(jnp.int32, (BLK, BLK), 0)
        ci = lax.broadcasted_iota(jnp.int32, (BLK, BLK), 1)
        sh = seg.bit_length() - 1
        rowseg = ri >> sh
        mask = ci <= ri
        if nseg > 1:
            mask = mask & (rowseg == (ci >> sh))
        r0 = bi * BLK
        rsl = slice(r0, r0 + BLK)
        seq0 = r0 // tm
        return rowseg, mask, rsl, slice(seq0, seq0 + nseg)

    def block_scores(bi):
        rowseg, mask, rsl, _ = block_consts(bi)
        hsl = [slice(h * M_DK, (h + 1) * M_DK) for h in range(M_HEADS)]
        q = [r.q_s[rsl, hs] for hs in hsl]
        k = [r.k_s[rsl, hs] for hs in hsl]
        vv = [r.v_s[rsl, hs] for hs in hsl]
        s = [lax.dot_general(q[h], k[h], (((1,), (1,)), ((), ())), preferred_element_type=F32)
             for h in range(M_HEADS)]
        sw = [s[h] * jnp.exp(jnp.where(mask, r.p_s[rsl, h:h + 1] + r.rt_s[h:h + 1, rsl], -jnp.inf))
              for h in range(M_HEADS)]
        kw = [k[h].astype(F32) * r.we_s[rsl, h:h + 1] for h in range(M_HEADS)]
        blk = types.SimpleNamespace(q=q)
        blk.rowsum = [jnp.sum(sw[h], axis=-1, keepdims=True) for h in range(M_HEADS)]
        blk.intra = [jnp.dot(sw[h].astype(BF16), vv[h], preferred_element_type=F32) for h in range(M_HEADS)]
        blk.upd = [lax.dot_general(block_diag(kw[h].astype(BF16), rowseg), vv[h], (((0,), (0,)), ((), ())),
                                   preferred_element_type=F32) for h in range(M_HEADS)]
        blk.ksum = [jnp.sum(kw[h].reshape(nseg, seg, M_DK), axis=1, keepdims=True) for h in range(M_HEADS)]
        v.blk[bi] = blk

    def block_state(bi):
        rowseg, _, rsl, ssl = block_consts(bi)
        blk = v.blk.pop(bi)
        g_blk = r.g_s[rsl, :].reshape(nseg, seg, LANES)
        heads = range(M_HEADS)
        c0 = [c_in[ssl, h] for h in heads]
        n0 = [n_in[ssl, h:h + 1, :] for h in heads]
        inter = [jnp.dot(block_diag(blk.q[h], rowseg), c0[h].astype(BF16).reshape(nseg * M_DK, M_DK),
                         preferred_element_type=F32) for h in heads]
        qn = [jnp.sum(blk.q[h].astype(F32) * jnp.broadcast_to(n0[h], (nseg, seg, M_DK)).reshape(BLK, M_DK),
                      axis=-1, keepdims=True) for h in heads]
        gc = [r.g_s[rsl, h:h + 1] for h in heads]
        num = [gc[h] * inter[h] + blk.intra[h] for h in heads]
        den = [gc[h] * qn[h] + blk.rowsum[h] for h in heads]
        hh = [num[h] * (1.0 / jnp.maximum(jnp.abs(den[h]), r.em_s[rsl, h:h + 1])) for h in heads]
        for h in heads:
            r.hn_s[rsl, h * M_DK:(h + 1) * M_DK] = _rms_rows(hh[h])
        for h in heads:
            decay = g_blk[:, seg - 1:seg, h:h + 1]
            r.c[ssl, h] = decay * c0[h] + blk.upd[h].reshape(nseg, M_DK, M_DK)
            r.n[ssl, h:h + 1, :] = decay * n0[h] + blk.ksum[h]

    def conv_inputs():
        if lo == 0:
            r.ubuf_s[:, CONV_ROW0 - CONV_W + 1:CONV_ROW0, :] = cv_in[...]
        u = proj(OFF_C, S_WIDTH) * proj(OFF_X, S_WIDTH)
        r.ubuf_s[:, CONV_ROW0 + lo:CONV_ROW0 + lo + th, :] = u.reshape(nb, th, S_WIDTH)
        r.cv[...] = r.ubuf_s[:, CONV_ROW0 + lo + th - CONV_W + 1:CONV_ROW0 + lo + th, :]

    def conv_outputs():
        bg = proj(OFF_B, S_WIDTH)
        cw = r.cw[...]
        first = CONV_ROW0 + lo - CONV_W + 1
        yc = cw[0:1] * r.ubuf_s[:, first:first + th, :]
        for j in range(1, CONV_W):
            yc = yc + cw[j:j + 1] * r.ubuf_s[:, first + j:first + j + th, :]
        ysv = (bg.reshape(nb, th, S_WIDTH) * yc).reshape(rows, S_WIDTH)
        for grp in range(S_GROUPS):
            cs = slice(M_WIDTH + grp * LANES, M_WIDTH + (grp + 1) * LANES)
            r.mix_s[fsl, cs] = (_rms_rows(ysv[:, grp * LANES:(grp + 1) * LANES]) * r.gmix[:, cs]).astype(BF16)

    def conv_half_out():
        v.y_conv = jnp.dot(r.mix_s[fsl, M_WIDTH:], r.wout_s[M_WIDTH:, :], preferred_element_type=F32)

    def finish():
        r.mix_s[fsl, :M_WIDTH] = (r.hn_s[fsl, :] * r.og_s[fsl, :] * r.gmix[:, :M_WIDTH]).astype(BF16)
        y = v.y_conv + jnp.dot(r.mix_s[fsl, :M_WIDTH], r.wout_s[:M_WIDTH, :], preferred_element_type=F32)
        r.xo[:, tsl, :] = v.x3 + r.gt[...] * y.reshape(nb, th, D_MODEL)

    blocks = [functools.partial(block_scores, blk0)]
    for bi in range(blk0, blk0 + nblk):
        if bi + 1 < blk0 + nblk:
            blocks.append(functools.partial(block_scores, bi + 1))
        blocks.append(functools.partial(block_state, bi))
    return types.SimpleNamespace(
        pre=[norm_gates, q_proj, gate_scalars, k_proj, v_proj], blocks=blocks,
        fillers=[o_proj, conv_inputs, conv_outputs, conv_half_out], finish=finish)


def _spread(chain, others):
    out, done = [], 0
    for i, step in enumerate(chain):
        out.append(step)
        want = ((i + 1) * len(others)) // len(chain)
        out.extend(others[done:want])
        done = want
    return out


def _mixer_steps(r, *, nb, tm, seg, first_tile, parts):
    th = tm // parts
    part = [_mixer_part(r, nb=nb, tm=tm, seg=seg, first_tile=first_tile, lo=i * th, th=th) for i in range(parts)]
    order = list(part[0].pre)
    for i, p in enumerate(part):
        others = list(part[i + 1].pre) if i + 1 < parts else []
        if i > 0:
            others += part[i - 1].fillers[2:] + [part[i - 1].finish]
        others += p.fillers[:2]
        order += _spread(p.blocks, others)
    return order + part[-1].fillers[2:] + [part[-1].finish]


_MIXER_REFS = (
    "x", "sh", "sc", "gt", "g1", "win", "bin", "cw", "gmix", "wout", "c0", "n0", "m0", "cv0",
    "xo", "c", "n", "m", "cv",
    "q_s", "k_s", "v_s", "og_s", "hn_s", "mix_s", "ubuf_s", "p_s", "g_s", "we_s", "em_s", "rt_s", "wout_s",
)


def _mixer_kernel(*refs, nb, tm, seg, nt, parts, n_cast):
    n_in, n_out = _MIXER_REFS.index("xo"), _MIXER_REFS.index("q_s")
    cast_in = refs[n_in:n_in + n_cast]
    cast_out = refs[n_out + n_cast:n_out + 2 * n_cast]
    refs = refs[:n_in] + refs[n_in + n_cast:n_out + n_cast] + refs[n_out + 2 * n_cast:]
    r = types.SimpleNamespace(**dict(zip(_MIXER_REFS, refs, strict=True)))
    first_tile = None if nt == 1 else pl.program_id(1) == 0

    @pl.when((pl.program_id(0) == 0) & (pl.program_id(1) == 0))
    def _():
        r.wout_s[...] = r.wout[...].astype(BF16)

    for step in _mixer_steps(r, nb=nb, tm=tm, seg=seg, first_tile=first_tile, parts=parts):
        step()
    for src_ref, dst_ref in zip(cast_in, cast_out):
        dst_ref[...] = src_ref[...].astype(BF16)


def _mixer(x, mod, mod_row0, layer, w, state, *, nb, tm, parts, name, cast=()):
    bsz, t_len, _ = x.shape
    rows = nb * tm
    seg = min(BLK, tm // parts)
    assert (rows // parts) % BLK == 0 and bsz % nb == 0 and t_len % tm == 0 and mod_row0 % nb == 0
    assert tm & (tm - 1) == 0 and (nb == 1 or tm == t_len)
    c0, n0, m0, cv0, st_layer = state
    grid = (bsz // nb, t_len // tm)
    mrow = mod_row0 // nb

    def mod_spec(j):
        return pl.BlockSpec((None, None, nb, 1, D_MODEL), lambda b, t: (layer, j, mrow + b, 0, 0))

    def wspec(shape, resident=False):
        nd = len(shape)
        mode = {"pipeline_mode": pl.Buffered(1)} if resident else {}
        return pl.BlockSpec((None,) + shape, lambda b, t: (layer,) + (0,) * nd, **mode)

    def sspec(shape):
        nd = len(shape)
        return pl.BlockSpec((None, nb) + shape, lambda b, t: (st_layer, b) + (0,) * nd)

    def ospec(shape):
        nd = len(shape)
        return pl.BlockSpec((nb,) + shape, lambda b, t: (b,) + (0,) * nd)

    state_shapes = ((M_HEADS, M_DK, M_DK), (M_HEADS, M_DK), (1, LANES), (CONV_W - 1, S_WIDTH))
    in_specs = [
        pl.BlockSpec((nb, tm, D_MODEL), lambda b, t: (b, t, 0)),
        mod_spec(0), mod_spec(1), mod_spec(2),
        wspec((1, D_MODEL)),
        wspec((D_MODEL, N_Z), True),
        wspec((1, N_Z)),
        wspec((CONV_W, S_WIDTH)),
        wspec((1, D_MODEL)),
        wspec((D_MODEL, D_MODEL), True),
    ] + [sspec(s) for s in state_shapes]
    out_shape = (jax.ShapeDtypeStruct(x.shape, F32),) + tuple(
        jax.ShapeDtypeStruct((bsz,) + s, F32) for s in state_shapes)
    out_specs = (pl.BlockSpec((nb, tm, D_MODEL), lambda b, t: (b, t, 0)),) + tuple(ospec(s) for s in state_shapes)
    nstep = grid[0] * grid[1]
    for a in cast:
        slab = a.shape[1] // nstep
        assert a.shape[1] % nstep == 0 and slab % 16 == 0
        in_specs.append(pl.BlockSpec((None, slab, a.shape[2]), lambda b, t: (layer, b * grid[1] + t, 0)))
        out_shape += (jax.ShapeDtypeStruct(a.shape[1:], BF16),)
        out_specs += (pl.BlockSpec((slab, a.shape[2]), lambda b, t: (b * grid[1] + t, 0)),)
    scratch = [
        pltpu.VMEM((rows, M_WIDTH), BF16),
        pltpu.VMEM((rows, M_WIDTH), BF16),
        pltpu.VMEM((rows, M_WIDTH), BF16),
        pltpu.VMEM((rows, M_WIDTH), F32),
        pltpu.VMEM((rows, M_WIDTH), F32),
        pltpu.VMEM((rows, D_MODEL), BF16),
        pltpu.VMEM((nb, CONV_ROW0 + tm, S_WIDTH), F32),
        pltpu.VMEM((rows, LANES), F32),
        pltpu.VMEM((rows, LANES), F32),
        pltpu.VMEM((rows, LANES), F32),
        pltpu.VMEM((rows, LANES), F32),
        pltpu.VMEM((LANES, rows), F32),
        pltpu.VMEM((D_MODEL, D_MODEL), BF16),
    ]
    kern = functools.partial(_mixer_kernel, nb=nb, tm=tm, seg=seg, nt=t_len // tm, parts=parts, n_cast=len(cast))
    return pl.pallas_call(
        kern, out_shape=out_shape, grid=grid, in_specs=in_specs, out_specs=out_specs,
        scratch_shapes=scratch,
        compiler_params=pltpu.CompilerParams(dimension_semantics=("arbitrary", "arbitrary"),
                                             vmem_limit_bytes=VMEM_LIMIT),
        name=name,
    )(x, mod, mod, mod, w["g1"], w["w_in"], w["b_in"], w["conv_w"], w["g_mix"], w["w_out"],
      c0, n0, m0, cv0, *cast)


def _mlp_tile(x_ref, sh_ref, sc_ref, gt_ref, g2_ref, wup_ref, wdn_ref, gf_ref, xo_ref, *, final):
    nb, tm, _ = x_ref.shape
    rows = nb * tm
    x3 = x_ref[...]
    h3 = _rms_rows(x3) * (g2_ref[...] * (1.0 + sc_ref[...])) + sh_ref[...]
    hb = h3.reshape(rows, D_MODEL).astype(BF16)
    acc = jnp.zeros((rows, D_MODEL), F32)
    for c in range(D_FF // FF_CHUNK):
        lo = c * FF_CHUNK
        up = jnp.dot(hb, wup_ref[:, lo:lo + FF_CHUNK], preferred_element_type=F32)
        act = jnp.square(jnp.maximum(up, 0.0)).astype(BF16)
        acc = acc + jnp.dot(act, wdn_ref[lo:lo + FF_CHUNK, :], preferred_element_type=F32)
    xn = x3 + gt_ref[...] * acc.reshape(nb, tm, D_MODEL)
    if final:
        xn = _rms_rows(xn) * gf_ref[...]
    xo_ref[...] = xn


def _mlp_kernel(xp_ref, shp_ref, scp_ref, gtp_ref, xs_ref, shs_ref, scs_ref, gts_ref, g2_ref, wup_ref, wdn_ref,
                gf_ref, xop_ref, xos_ref, *, n_prompt, final):
    s = pl.program_id(0)
    shared = (g2_ref, wup_ref, wdn_ref, gf_ref)

    @pl.when(s < n_prompt)
    def _():
        _mlp_tile(xp_ref, shp_ref, scp_ref, gtp_ref, *shared, xop_ref, final=final)

    @pl.when(s == n_prompt)
    def _():
        _mlp_tile(xs_ref, shs_ref, scs_ref, gts_ref, *shared, xos_ref, final=final)


def _mlp(xp, xs, mod, mod_row0, layer, w, w_up, w_down, g_final, *, tm, final, name):
    bp, t_len, _ = xp.shape
    bs, ss, _ = xs.shape
    assert t_len % tm == 0
    nt = t_len // tm
    n_prompt = bp * nt

    def ptile(s):
        return jnp.minimum(s, n_prompt - 1)

    def pmod(j):
        return pl.BlockSpec((None, None, 1, 1, D_MODEL), lambda s: (layer, j, mod_row0 + ptile(s) // nt, 0, 0))

    def smod(j):
        return pl.BlockSpec((None, None, bs, 1, D_MODEL), lambda s: (layer, j, 0, 0, 0))

    xp_spec = pl.BlockSpec((1, tm, D_MODEL), lambda s: (ptile(s) // nt, ptile(s) % nt, 0))
    in_specs = [
        xp_spec, pmod(3), pmod(4), pmod(5),
        pl.BlockSpec((bs, ss, D_MODEL), lambda s: (0, 0, 0), pipeline_mode=pl.Buffered(1)), smod(3), smod(4), smod(5),
        pl.BlockSpec((None, 1, D_MODEL), lambda s: (layer, 0, 0)),
        pl.BlockSpec((D_MODEL, D_FF), lambda s: (0, 0), pipeline_mode=pl.Buffered(1)),
        pl.BlockSpec((D_FF, D_MODEL), lambda s: (0, 0), pipeline_mode=pl.Buffered(1)),
        pl.BlockSpec((1, D_MODEL), lambda s: (0, 0)),
    ]
    return pl.pallas_call(
        functools.partial(_mlp_kernel, n_prompt=n_prompt, final=final),
        out_shape=(jax.ShapeDtypeStruct(xp.shape, F32), jax.ShapeDtypeStruct(xs.shape, F32)),
        grid=(n_prompt + 1,), in_specs=in_specs,
        out_specs=(xp_spec, pl.BlockSpec((bs, ss, D_MODEL), lambda s: (0, 0, 0))),
        compiler_params=pltpu.CompilerParams(dimension_semantics=("arbitrary",), vmem_limit_bytes=VMEM_LIMIT),
        name=name,
    )(xp, mod, mod, mod, xs, mod, mod, mod, w["g2"], w_up, w_down, g_final)


def kernel(x_prompt, x_sample, c_prompt, c_sample, state_C, state_n, state_m, state_conv, w_ada, b_ada,
           g_norm1, w_in, b_in, conv_w, g_mix_out, w_out, g_norm2, w_up, w_down, g_final):
    bp, sp, _ = x_prompt.shape
    bs, ss, _ = x_sample.shape

    def relayout_bias(a):
        gi = a[..., REF_OFF_I:REF_OFF_F]
        gf = a[..., REF_OFF_F:REF_OFF_B]
        pad = [(0, 0)] * (a.ndim - 1) + [(0, LANES - M_HEADS)]
        return jnp.concatenate([a[..., :REF_OFF_I], a[..., REF_OFF_B:], jnp.pad(gi, pad), jnp.pad(gf, pad)], axis=-1)

    w = {
        "g1": g_norm1.reshape(DEPTH, 1, D_MODEL),
        "w_in": _relayout_w_in(w_in),
        "b_in": relayout_bias(b_in).reshape(DEPTH, 1, N_Z),
        "conv_w": conv_w,
        "g_mix": g_mix_out.reshape(DEPTH, 1, D_MODEL),
        "w_out": w_out,
        "g2": g_norm2.reshape(DEPTH, 1, D_MODEL),
    }
    gfin = g_final.reshape(1, D_MODEL)

    mod = _modulation(jnp.concatenate([c_sample, c_prompt], axis=0), w_ada, b_ada)

    zero_state = (jnp.zeros((1, bp, M_HEADS, M_DK, M_DK), F32), jnp.zeros((1, bp, M_HEADS, M_DK), F32),
                  jnp.zeros((1, bp, 1, LANES), F32), jnp.zeros((1, bp, CONV_W - 1, S_WIDTH), F32))
    m_pad = jnp.pad(state_m, ((0, 0), (0, 0), (0, LANES - M_HEADS))).reshape(DEPTH, bs, 1, LANES)

    tm_p = min(PROMPT_TILE_ROWS, sp)
    nb_s = min(SAMPLE_SEQS_PER_STEP, bs)
    xp, xs = x_prompt, x_sample
    outs = [[] for _ in range(8)]
    for l in range(DEPTH):
        last = l == DEPTH - 1
        xp, c1, n1, m1, cv1, wu, wd = _mixer(xp, mod, bs, l, w, zero_state + (0,), nb=1, tm=tm_p, parts=PROMPT_PARTS,
                                             name=f"mixer_prompt_{l}", cast=(w_up, w_down))
        xs, c2, n2, m2, cv2 = _mixer(xs, mod, 0, l, w, (state_C, state_n, m_pad, state_conv, l),
                                     nb=nb_s, tm=ss, parts=1, name=f"mixer_sample_{l}")
        for lst, val in zip(outs, (c1, n1, m1[:, 0, :M_HEADS], cv1, c2, n2, m2[:, 0, :M_HEADS], cv2)):
            lst.append(val)
        xp, xs = _mlp(xp, xs, mod, bs, l, w, wu, wd, gfin, tm=tm_p, final=last, name=f"mlp_{l}")
    return (xp, xs) + tuple(jnp.stack(o) for o in outs)
```
